```python
import math
import jax, jax.numpy as jnp
from jax import lax
import numpy as np

D_MODEL = 1024
BATCH = 2
SEQ = 8192
DEPTH = 2
DEC_BATCH = 32
DEC_SEQ = 2048
PAST_LEN = 128

GRID_W = 64
N_MIXERS = 2
N_LAYERS_A = (DEPTH + 1) // 2
N_LAYERS_B = DEPTH // 2
RMS_EPS = 1e-6
L2_EPS = 1e-6
NEG_INF = -1e30

NA_HEADS = 16
NA_HEAD_DIM = D_MODEL // NA_HEADS
NA_WIDTH = NA_HEADS * NA_HEAD_DIM
NA_WIN_H = 8
NA_WIN_W = 16
NA_QBLOCK_W = 16
NA_KBLOCK_W = NA_QBLOCK_W + NA_WIN_W
NA_N_CBLOCKS = GRID_W // NA_QBLOCK_W
NA_IN = 4 * NA_WIDTH

GDN_HEADS = 8
GDN_DK = 128
GDN_DV = 256
GDN_KW = GDN_HEADS * GDN_DK
GDN_VW = GDN_HEADS * GDN_DV
GDN_CONV = 5
GDN_CONV_CH = 2 * GDN_KW + GDN_VW
GDN_CHUNK = 64
GDN_IN = 2 * GDN_KW + 2 * GDN_VW + 4 * GDN_HEADS

kernel_name = "hybrid_natten_gdn_encoder"


def _rmsnorm(x, w):
    xf = x.astype(jnp.float32)
    y = xf * lax.rsqrt(jnp.mean(jnp.square(xf), axis=-1, keepdims=True) + RMS_EPS)
    return (y * w.astype(jnp.float32)).astype(x.dtype)


def _l2norm(x):
    xf = x.astype(jnp.float32)
    return xf * lax.rsqrt(jnp.sum(jnp.square(xf), axis=-1, keepdims=True) + L2_EPS)


def _na_column_tables():
    j = np.arange(NA_N_CBLOCKS)
    slab_start = np.clip(j * NA_QBLOCK_W - NA_WIN_W // 2, 0, GRID_W - NA_KBLOCK_W)
    key_col = slab_start[:, None] + np.arange(NA_KBLOCK_W)[None, :]
    q_col = j[:, None] * NA_QBLOCK_W + np.arange(NA_QBLOCK_W)[None, :]
    win_start = np.clip(q_col - NA_WIN_W // 2, 0, GRID_W - NA_WIN_W)
    rel = key_col[:, None, :] - win_start[:, :, None]
    mask = (rel >= 0) & (rel < NA_WIN_W)
    dc = key_col[:, None, :] - q_col[:, :, None]
    dc_idx = np.clip(dc + NA_WIN_W - 1, 0, 2 * NA_WIN_W - 2)
    return key_col, mask, dc_idx


def _neighbourhood_attention(h, w_in, rpb, w_out):
    b, t, _ = h.shape
    rows = t // GRID_W
    kh = min(NA_WIN_H, rows)
    proj = h @ w_in
    q, k, v, g = jnp.split(proj, 4, axis=-1)
    grid = lambda z: z.reshape(b, rows, GRID_W, NA_HEADS, NA_HEAD_DIM)
    q = grid(q) * (NA_HEAD_DIM ** -0.5)
    k = grid(k)
    v = grid(v)
    key_col, col_mask, dc_idx = _na_column_tables()
    rpb_cols = rpb.astype(jnp.float32)[:, :, dc_idx]
    mask = col_mask[:, :, None, :]

    def one_row(r):
        rs = jnp.clip(r - kh // 2, 0, rows - kh)
        q_r = lax.dynamic_index_in_dim(q, r, axis=1, keepdims=False)
        q_r = q_r.reshape(b, NA_N_CBLOCKS, NA_QBLOCK_W, NA_HEADS, NA_HEAD_DIM)
        k_s = lax.dynamic_slice_in_dim(k, rs, kh, axis=1)[:, :, key_col]
        v_s = lax.dynamic_slice_in_dim(v, rs, kh, axis=1)[:, :, key_col]
        s = jnp.einsum('bjqhd,bijkhd->bhjqik', q_r, k_s).astype(jnp.float32)
        dr = rs + jnp.arange(kh) - r + (NA_WIN_H - 1)
        bias = jnp.take(rpb_cols, dr, axis=1).transpose(0, 2, 3, 1, 4)
        s = jnp.where(mask, s + bias, NEG_INF)
        p = jax.nn.softmax(s.reshape(b, NA_HEADS, NA_N_CBLOCKS, NA_QBLOCK_W, kh * NA_KBLOCK_W), axis=-1)
        p = p.reshape(s.shape).astype(v.dtype)
        o = jnp.einsum('bhjqik,bijkhd->bjqhd', p, v_s)
        return o.reshape(b, GRID_W, NA_WIDTH)

    o = lax.map(one_row, jnp.arange(rows))
    o = o.transpose(1, 0, 2, 3).reshape(b, t, NA_WIDTH)
    return (o * jax.nn.silu(g)) @ w_out


def _centred_depthwise_conv(x, w):
    pad = GDN_CONV // 2
    return lax.conv_general_dilated(x, w[:, None, :], window_strides=(1,), padding=[(pad, pad)],
                                    dimension_numbers=('NWC', 'WIO', 'NWC'),
                                    feature_group_count=x.shape[-1])


def _chunk_gated_delta(q, k, v, g, beta):
    b, t, h, dk = q.shape
    dv = v.shape[-1]
    c = GDN_CHUNK
    n = t // c
    ch = lambda z: z.transpose(0, 2, 1, 3).reshape(b, h, n, c, z.shape[-1])
    q, k, v = ch(q), ch(k), ch(v)
    beta = beta.transpose(0, 2, 1).reshape(b, h, n, c)
    g = jnp.cumsum(g.transpose(0, 2, 1).reshape(b, h, n, c), axis=-1)
    tril = np.tril(np.ones((c, c), dtype=bool))
    tril_strict = np.tril(np.ones((c, c), dtype=bool), -1)
    diff = g[..., :, None] - g[..., None, :]
    decay = jnp.where(tril, jnp.exp(jnp.where(tril, diff, 0.0)), 0.0)
    k_beta = k * beta[..., None]
    a_kk = jnp.where(tril_strict, jnp.einsum('bhncd,bhnsd->bhncs', k_beta, k) * decay, 0.0)
    lhs = jnp.eye(c, dtype=jnp.float32) + a_kk
    w = lax.linalg.triangular_solve(lhs, k_beta * jnp.exp(g)[..., None], left_side=True,
                                    lower=True, unit_diagonal=True)
    u = lax.linalg.triangular_solve(lhs, v * beta[..., None], left_side=True,
                                    lower=True, unit_diagonal=True)
    a_qk = jnp.einsum('bhncd,bhnsd->bhncs', q, k) * decay

    def step(state, xs):
        q_i, k_i, u_i, w_i, g_i, a_i = xs
        v_new = u_i - jnp.einsum('bhcd,bhde->bhce', w_i, state)
        o = jnp.einsum('bhcd,bhde->bhce', q_i * jnp.exp(g_i)[..., None], state) \
            + jnp.einsum('bhcs,bhse->bhce', a_i, v_new)
        g_last = g_i[..., -1]
        state = state * jnp.exp(g_last)[..., None, None] + jnp.einsum(
            'bhcd,bhce->bhde', k_i * jnp.exp(g_last[..., None] - g_i)[..., None], v_new)
        return state, o

    mv = lambda z: jnp.moveaxis(z, 2, 0)
    s0 = jnp.zeros((b, h, dk, dv), jnp.float32)
    _, o = lax.scan(step, s0, (mv(q), mv(k), mv(u), mv(w), mv(g), mv(a_qk)))
    return jnp.moveaxis(o, 0, 2).reshape(b, h, t, dv).transpose(0, 2, 1, 3)


def _gated_deltanet(h, w_in, conv_w, a_log, dt_bias, norm_w, w_out):
    b, t, _ = h.shape
    proj = h @ w_in
    o1 = GDN_CONV_CH
    o2 = o1 + GDN_VW
    o3 = o2 + 2 * GDN_HEADS
    qkv = jax.nn.silu(_centred_depthwise_conv(proj[..., :o1], conv_w))
    z = proj[..., o1:o2].reshape(b, t, GDN_HEADS, GDN_DV)
    beta = jax.nn.sigmoid(proj[..., o2:o3].astype(jnp.float32)).reshape(b, t, 2, GDN_HEADS)
    a_in = proj[..., o3:].astype(jnp.float32).reshape(b, t, 2, GDN_HEADS)
    g = -jnp.exp(a_log.astype(jnp.float32)) * jax.nn.softplus(a_in + dt_bias.astype(jnp.float32))
    q = _l2norm(qkv[..., :GDN_KW].reshape(b, t, GDN_HEADS, GDN_DK)) * (GDN_DK ** -0.5)
    k = _l2norm(qkv[..., GDN_KW:2 * GDN_KW].reshape(b, t, GDN_HEADS, GDN_DK))
    v = qkv[..., 2 * GDN_KW:].reshape(b, t, GDN_HEADS, GDN_DV).astype(jnp.float32)
    fl = lambda x: jnp.flip(x, axis=1)
    o_fwd = _chunk_gated_delta(q, k, v, g[:, :, 0], beta[:, :, 0])
    o_bwd = _chunk_gated_delta(fl(q), fl(k), fl(v), fl(g[:, :, 1]), fl(beta[:, :, 1]))
    o = (o_fwd + fl(o_bwd)).astype(h.dtype)
    o = _rmsnorm(o, norm_w) * jax.nn.silu(z)
    return o.reshape(b, t, GDN_VW) @ w_out


def _trunk(x, ln_w, na_w_in, na_rpb, na_w_out, gdn_w_in, gdn_conv_w, gdn_a_log, gdn_dt_bias,
           gdn_norm_w, gdn_w_out, final_norm_w):
    h = x
    for i in range(DEPTH):
        hn = _rmsnorm(h, ln_w[i])
        li = i // N_MIXERS
        if i % N_MIXERS == 0:
            h = h + _neighbourhood_attention(hn, na_w_in[li], na_rpb[li], na_w_out[li])
        else:
            h = h + _gated_deltanet(hn, gdn_w_in[li], gdn_conv_w[li], gdn_a_log[li], gdn_dt_bias[li],
                                    gdn_norm_w[li], gdn_w_out[li])
    return _rmsnorm(h, final_norm_w)


def setup_inputs(seed: int = 0) -> dict:
    key = jax.random.key(seed)
    ks = jax.random.split(key, 13)
    f32 = jnp.float32
    nrm = lambda k, shape, scale: jax.random.normal(k, shape, f32) * scale
    x_prompt = nrm(ks[0], (BATCH, SEQ, D_MODEL), 1.0)
    x_sample = nrm(ks[1], (DEC_BATCH, DEC_SEQ, D_MODEL), 1.0)
    ln_w = 1.0 + nrm(ks[2], (DEPTH, D_MODEL), 0.02)
    na_w_in = nrm(ks[3], (N_LAYERS_A, D_MODEL, NA_IN), D_MODEL ** -0.5)
    na_rpb = nrm(ks[4], (N_LAYERS_A, NA_HEADS, 2 * NA_WIN_H - 1, 2 * NA_WIN_W - 1), 0.05)
    na_w_out = nrm(ks[5], (N_LAYERS_A, NA_WIDTH, D_MODEL), NA_WIDTH ** -0.5)
    gdn_w_in = nrm(ks[6], (N_LAYERS_B, D_MODEL, GDN_IN), D_MODEL ** -0.5)
    gdn_conv_w = nrm(ks[7], (N_LAYERS_B, GDN_CONV, GDN_CONV_CH), GDN_CONV ** -0.5)
    gdn_a_log = jnp.log(jax.random.uniform(ks[8], (N_LAYERS_B, 2, GDN_HEADS), f32, 1.0, 16.0))
    dt = jnp.exp(jax.random.uniform(ks[9], (N_LAYERS_B, 2, GDN_HEADS), f32,
                                    math.log(1e-3), math.log(1e-1)))
    gdn_dt_bias = dt + jnp.log(-jnp.expm1(-dt))
    gdn_norm_w = 1.0 + nrm(ks[10], (N_LAYERS_B, GDN_DV), 0.02)
    gdn_w_out = nrm(ks[11], (N_LAYERS_B, GDN_VW, D_MODEL), GDN_VW ** -0.5)
    final_norm_w = 1.0 + nrm(ks[12], (D_MODEL,), 0.02)
    return {"x_prompt": x_prompt, "x_sample": x_sample, "ln_w": ln_w, "na_w_in": na_w_in,
            "na_rpb": na_rpb, "na_w_out": na_w_out, "gdn_w_in": gdn_w_in, "gdn_conv_w": gdn_conv_w,
            "gdn_a_log": gdn_a_log, "gdn_dt_bias": gdn_dt_bias, "gdn_norm_w": gdn_norm_w,
            "gdn_w_out": gdn_w_out, "final_norm_w": final_norm_w}


def reference(x_prompt, x_sample, ln_w, na_w_in, na_rpb, na_w_out, gdn_w_in, gdn_conv_w, gdn_a_log,
              gdn_dt_bias, gdn_norm_w, gdn_w_out, final_norm_w):
    y_prompt = _trunk(x_prompt, ln_w, na_w_in, na_rpb, na_w_out, gdn_w_in, gdn_conv_w, gdn_a_log,
                      gdn_dt_bias, gdn_norm_w, gdn_w_out, final_norm_w)
    y_sample = _trunk(x_sample, ln_w, na_w_in, na_rpb, na_w_out, gdn_w_in, gdn_conv_w, gdn_a_log,
                      gdn_dt_bias, gdn_norm_w, gdn_w_out, final_norm_w)
    return (y_prompt, y_sample)
```

```python
import functools

import numpy as np
import jax
import jax.numpy as jnp
from jax import lax
from jax.experimental import pallas as pl
from jax.experimental.pallas import tpu as pltpu

F32 = jnp.float32
BF16 = jnp.bfloat16

D_MODEL = 1024
GRID_W = 64
RMS_EPS = 1e-6
L2_EPS = 1e-6
NEG_INF = -1e30

NA_HEADS = 16
NA_HEAD_DIM = 64
NA_WIDTH = NA_HEADS * NA_HEAD_DIM
NA_WIN_H = 8
NA_WIN_W = 16
NA_ROWS_PER_BLOCK = 4
NA_WIN_TOKENS = NA_WIN_H * GRID_W

GDN_HEADS = 8
GDN_DK = 128
GDN_DV = 256
GDN_KW = GDN_HEADS * GDN_DK
GDN_VW = GDN_HEADS * GDN_DV
GDN_CONV = 5
GDN_CONV_CH = 2 * GDN_KW + GDN_VW
GDN_CHUNK = 64
GDN_GATE_LANES = 128
GDN_BLOCK_TOKENS = 256
CONV_HALO = 16

TOKEN_TILE = 512
VMEM_LIMIT = 56 * 1024 * 1024


def _cparams(sem):
    return pltpu.CompilerParams(dimension_semantics=sem, vmem_limit_bytes=VMEM_LIMIT)


def _rms_scale(x):
    return lax.rsqrt(jnp.mean(x * x, axis=-1, keepdims=True) + RMS_EPS)


def _silu(x):
    return x * jax.nn.sigmoid(x)


def _const_spec(shape):
    nd = len(shape)
    return pl.BlockSpec(shape, lambda *_: (0,) * nd, pipeline_mode=pl.Buffered(1))


def _na_in_kernel(x_ref, lnw_ref, w_ref, q_ref, k_ref, v_ref, g_ref):
    x = x_ref[...]
    xn = (x * _rms_scale(x) * lnw_ref[...]).astype(BF16)
    outs = (q_ref, k_ref, v_ref, g_ref)
    for j, o_ref in enumerate(outs):
        y = jnp.dot(xn, w_ref[:, j * NA_WIDTH:(j + 1) * NA_WIDTH], preferred_element_type=F32)
        if j == 0:
            y = y * (NA_HEAD_DIM ** -0.5)
        o_ref[...] = y.astype(o_ref.dtype)


def _na_in_proj(x2, lnw, w):
    n = x2.shape[0]
    tm = TOKEN_TILE
    tok = lambda width: pl.BlockSpec((tm, width), lambda i: (i, 0))
    out = jax.ShapeDtypeStruct((n, NA_WIDTH), BF16)
    return pl.pallas_call(
        _na_in_kernel,
        grid=(n // tm,),
        in_specs=[tok(D_MODEL), _const_spec((1, D_MODEL)), _const_spec((D_MODEL, 4 * NA_WIDTH))],
        out_specs=[tok(NA_WIDTH)] * 4,
        out_shape=[out] * 4,
        compiler_params=_cparams(("parallel",)),
        name="na_in_proj",
    )(x2, lnw, w)


def _na_bias_tables(rpb, rows):
    qc = np.arange(GRID_W)[:, None]
    kc = np.arange(GRID_W)[None, :]
    win_start = np.clip(qc - NA_WIN_W // 2, 0, GRID_W - NA_WIN_W)
    col_ok = (kc >= win_start) & (kc < win_start + NA_WIN_W)
    dc_idx = np.clip(kc - qc + NA_WIN_W - 1, 0, 2 * NA_WIN_W - 2)
    rb = NA_ROWS_PER_BLOCK
    d_first = np.arange(rb)
    d_mid = np.full((rb,), NA_WIN_H // 2)
    d_last = NA_WIN_H - rb + np.arange(rb)
    d_all = np.concatenate([d_first, d_mid, d_last])
    dr = np.arange(NA_WIN_H)[None, :] - d_all[:, None] + (NA_WIN_H - 1)
    b = rpb.astype(F32)[:, dr]
    b = b[:, :, :, dc_idx]
    b = jnp.where(col_ok[None, None, None], b, NEG_INF)
    b = b.transpose(1, 0, 3, 2, 4)
    return b.reshape(3 * rb, NA_HEADS, GRID_W, NA_WIN_TOKENS)


def _na_kernel(q_ref, g_ref, kp_ref, kc_ref, kn_ref, vp_ref, vc_ref, vn_ref, bias_ref,
               o_ref, kcat_ref, vcat_ref, *, rows):
    rb = NA_ROWS_PER_BLOCK
    blk_tokens = rb * GRID_W
    i = pl.program_id(0)
    kcat_ref[0:blk_tokens, :] = kp_ref[0]
    kcat_ref[blk_tokens:2 * blk_tokens, :] = kc_ref[0]
    kcat_ref[2 * blk_tokens:3 * blk_tokens, :] = kn_ref[0]
    vcat_ref[0:blk_tokens, :] = vp_ref[0]
    vcat_ref[blk_tokens:2 * blk_tokens, :] = vc_ref[0]
    vcat_ref[2 * blk_tokens:3 * blk_tokens, :] = vn_ref[0]

    lane = lax.broadcasted_iota(jnp.int32, (GRID_W, 2 * NA_HEAD_DIM), 1)
    first_half = lane < NA_HEAD_DIM
    r0 = i * rb
    for qi in range(rb):
        r = r0 + qi
        rs = jnp.clip(r - NA_WIN_H // 2, 0, rows - NA_WIN_H)
        off = pl.multiple_of((rs - (r0 - rb)) * GRID_W, GRID_W)
        qrows = pl.ds(qi * GRID_W, GRID_W)
        for p in range(NA_HEADS // 2):
            lanes = pl.ds(p * 2 * NA_HEAD_DIM, 2 * NA_HEAD_DIM)
            qp = q_ref[0, qrows, lanes]
            kw = kcat_ref[pl.ds(off, NA_WIN_TOKENS), lanes]
            vw = vcat_ref[pl.ds(off, NA_WIN_TOKENS), lanes]
            zero = jnp.zeros_like(qp)
            o_half = []
            for half in range(2):
                qh = jnp.where(first_half, qp, zero) if half == 0 else jnp.where(first_half, zero, qp)
                s = lax.dot_general(qh, kw, (((1,), (1,)), ((), ())), preferred_element_type=F32)
                s = s + bias_ref[qi, 2 * p + half]
                m = jnp.max(s, axis=-1, keepdims=True)
                e = jnp.exp(s - m)
                l = jnp.sum(e, axis=-1, keepdims=True)
                o = jnp.dot(e.astype(BF16), vw, preferred_element_type=F32)
                o_half.append(o / l)
            o = jnp.where(first_half, o_half[0], o_half[1])
            g = g_ref[0, qrows, lanes].astype(F32)
            o_ref[0, qrows, lanes] = (o * _silu(g)).astype(o_ref.dtype)


def _neighbourhood_attention(q, k, v, g, bias):
    b, t, _ = q.shape
    rows = t // GRID_W
    rb = NA_ROWS_PER_BLOCK
    assert rows % rb == 0 and rows >= 2 * NA_WIN_H
    nb = rows // rb
    bt = rb * GRID_W
    cur = pl.BlockSpec((1, bt, NA_WIDTH), lambda i, bi: (bi, i, 0))
    prev = pl.BlockSpec((1, bt, NA_WIDTH), lambda i, bi: (bi, jnp.maximum(i - 1, 0), 0))
    nxt = pl.BlockSpec((1, bt, NA_WIDTH), lambda i, bi: (bi, jnp.minimum(i + 1, nb - 1), 0))
    case = lambda i: jnp.where(i == 0, 0, jnp.where(i == nb - 1, 2, 1))
    bias_spec = pl.BlockSpec((rb, NA_HEADS, GRID_W, NA_WIN_TOKENS), lambda i, bi: (case(i), 0, 0, 0))
    return pl.pallas_call(
        functools.partial(_na_kernel, rows=rows),
        grid=(nb, b),
        in_specs=[cur, cur, prev, cur, nxt, prev, cur, nxt, bias_spec],
        out_specs=cur,
        out_shape=jax.ShapeDtypeStruct((b, t, NA_WIDTH), BF16),
        scratch_shapes=[pltpu.VMEM((3 * bt, NA_WIDTH), BF16), pltpu.VMEM((3 * bt, NA_WIDTH), BF16)],
        compiler_params=_cparams(("arbitrary", "arbitrary")),
        name="neighbourhood_attention",
    )(q, g, k, k, k, v, v, v, bias)


def _mid_kernel(x_ref, og_ref, wo_ref, lnw_ref, wqkv_ref, wz_ref, wg_ref,
                h_ref, qkv_ref, z_ref, gate_ref):
    h = x_ref[...] + jnp.dot(og_ref[...], wo_ref[...], preferred_element_type=F32)
    h_ref[...] = h
    hn = (h * _rms_scale(h) * lnw_ref[...]).astype(BF16)
    qkv_ref[...] = jnp.dot(hn, wqkv_ref[...], preferred_element_type=F32).astype(qkv_ref.dtype)
    z_ref[...] = jnp.dot(hn, wz_ref[...], preferred_element_type=F32).astype(z_ref.dtype)
    gate_ref[...] = jnp.dot(hn, wg_ref[...], preferred_element_type=F32)


def _mid_proj(x2, og2, wo, lnw, wqkv, wz, wg):
    n = x2.shape[0]
    tm = TOKEN_TILE // 2
    tok = lambda width: pl.BlockSpec((tm, width), lambda i: (i, 0))
    return pl.pallas_call(
        _mid_kernel,
        grid=(n // tm,),
        in_specs=[tok(D_MODEL), tok(NA_WIDTH), _const_spec(wo.shape), _const_spec((1, D_MODEL)),
                  _const_spec(wqkv.shape), _const_spec(wz.shape), _const_spec(wg.shape)],
        out_specs=[tok(D_MODEL), tok(GDN_CONV_CH), tok(GDN_VW), tok(GDN_GATE_LANES)],
        out_shape=[jax.ShapeDtypeStruct((n, D_MODEL), F32),
                   jax.ShapeDtypeStruct((n, GDN_CONV_CH), BF16),
                   jax.ShapeDtypeStruct((n, GDN_VW), BF16),
                   jax.ShapeDtypeStruct((n, GDN_GATE_LANES), F32)],
        compiler_params=_cparams(("parallel",)),
        name="mid_proj",
    )(x2, og2, wo, lnw, wqkv, wz, wg)


def _conv_kernel(prev_ref, cur_ref, next_ref, cw_ref, graw_ref, alog_ref, dtb_ref,
                 q_ref, k_ref, v_ref, gate_ref, ext_ref):
    i = pl.program_id(1)
    nblk = pl.num_programs(1)
    tb = GDN_BLOCK_TOKENS
    pad = GDN_CONV // 2
    halo = 8
    before = prev_ref[0].astype(F32)[CONV_HALO - halo:, :]
    after = next_ref[0].astype(F32)[:halo, :]
    ext_ref[0:halo, :] = jnp.where(i > 0, before, 0.0)
    ext_ref[halo:halo + tb, :] = cur_ref[0].astype(F32)
    ext_ref[halo + tb:, :] = jnp.where(i < nblk - 1, after, 0.0)

    def conv_silu(lo, hi):
        acc = None
        for j in range(GDN_CONV):
            term = ext_ref[halo - pad + j:halo - pad + j + tb, lo:hi] * cw_ref[j:j + 1, lo:hi]
            acc = term if acc is None else acc + term
        return _silu(acc)

    for h in range(GDN_HEADS):
        for base, o_ref, scale in ((0, q_ref, GDN_DK ** -0.5), (GDN_KW, k_ref, 1.0)):
            y = conv_silu(base + h * GDN_DK, base + (h + 1) * GDN_DK)
            y = y * (lax.rsqrt(jnp.sum(y * y, axis=-1, keepdims=True) + L2_EPS) * scale)
            o_ref[0, :, h * GDN_DK:(h + 1) * GDN_DK] = y.astype(o_ref.dtype)
    for c in range(GDN_VW // 256):
        y = conv_silu(2 * GDN_KW + c * 256, 2 * GDN_KW + (c + 1) * 256)
        v_ref[0, :, c * 256:(c + 1) * 256] = y.astype(v_ref.dtype)

    x = graw_ref[0]
    lane = lax.broadcasted_iota(jnp.int32, x.shape, 1)
    beta = jax.nn.sigmoid(x)
    decay = -jnp.exp(alog_ref[...]) * jax.nn.softplus(x + dtb_ref[...])
    for d in range(2):
        bsh = beta if d == 0 else pltpu.roll(beta, GDN_GATE_LANES - d * GDN_HEADS, 1)
        gsh = pltpu.roll(decay, GDN_GATE_LANES - (1 + d) * GDN_HEADS, 1)
        gate_ref[d, 0] = jnp.where(lane < GDN_HEADS, bsh, jnp.where(lane < 2 * GDN_HEADS, gsh, 0.0))


def _conv_gates(qkv, graw, conv_w, alog_l, dtb_l):
    b, t, _ = qkv.shape
    tb = GDN_BLOCK_TOKENS
    nblk = t // tb
    hb = tb // CONV_HALO
    nh = t // CONV_HALO
    cur = lambda width: pl.BlockSpec((1, tb, width), lambda bi, i: (bi, i, 0))
    prev = pl.BlockSpec((1, CONV_HALO, GDN_CONV_CH), lambda bi, i: (bi, jnp.maximum(i * hb - 1, 0), 0))
    nxt = pl.BlockSpec((1, CONV_HALO, GDN_CONV_CH),
                       lambda bi, i: (bi, jnp.minimum((i + 1) * hb, nh - 1), 0))
    return pl.pallas_call(
        _conv_kernel,
        grid=(b, nblk),
        in_specs=[prev, cur(GDN_CONV_CH), nxt, _const_spec(conv_w.shape), cur(GDN_GATE_LANES),
                  _const_spec((1, GDN_GATE_LANES)), _const_spec((1, GDN_GATE_LANES))],
        out_specs=[cur(GDN_KW), cur(GDN_KW), cur(GDN_VW),
                   pl.BlockSpec((2, 1, tb, GDN_GATE_LANES), lambda bi, i: (0, bi, i, 0))],
        out_shape=[jax.ShapeDtypeStruct((b, t, GDN_KW), BF16),
                   jax.ShapeDtypeStruct((b, t, GDN_KW), BF16),
                   jax.ShapeDtypeStruct((b, t, GDN_VW), BF16),
                   jax.ShapeDtypeStruct((2, b, t, GDN_GATE_LANES), F32)],
        scratch_shapes=[pltpu.VMEM((tb + 16, GDN_CONV_CH), F32)],
        compiler_params=_cparams(("parallel", "arbitrary")),
        name="conv_gates",
    )(qkv, qkv, qkv, conv_w, graw, alog_l, dtb_l)


def _split3(x):
    hi = x.astype(BF16)
    r1 = x - hi.astype(F32)
    mid = r1.astype(BF16)
    lo = (r1 - mid.astype(F32)).astype(BF16)
    return hi, mid, lo


def _unit_triangular_inverse(a):
    c = a.shape[0]
    eye = (lax.broadcasted_iota(jnp.int32, (c, c), 0) == lax.broadcasted_iota(jnp.int32, (c, c), 1))
    x = jnp.where(eye, 1.0, 0.0) - a
    p = a
    for _ in range(int(np.log2(c)) - 1):
        pb = p.astype(BF16)
        p = jnp.dot(pb, pb, preferred_element_type=F32)
        x = x + jnp.dot(x.astype(BF16), p.astype(BF16), preferred_element_type=F32)
    return x


def _delta_kernel(q_ref, k_ref, v_ref, gate_ref, o_ref, s_ref):
    d = pl.program_id(0)
    i = pl.program_id(2)
    c = GDN_CHUNK
    n_chunks = GDN_BLOCK_TOKENS // c

    @pl.when(i == 0)
    def _():
        s_ref[...] = jnp.zeros_like(s_ref)

    sgn = jnp.where(d == 0, 1, -1)
    row = lax.broadcasted_iota(jnp.int32, (c, c), 0)
    col = lax.broadcasted_iota(jnp.int32, (c, c), 1)
    delta = (row - col) * sgn
    causal = delta >= 0
    strict = delta > 0
    tri = jnp.where(causal, 1.0, 0.0).astype(BF16)

    def chunk_body(ci, carry):
        cc = jnp.where(d == 0, ci, n_chunks - 1 - ci)
        rows = pl.ds(pl.multiple_of(cc * c, c), c)
        gates = gate_ref[0, 0, rows, :]
        hi, mid, lo = _split3(gates)
        gcum = (jnp.dot(tri, hi, preferred_element_type=F32)
                + jnp.dot(tri, mid, preferred_element_type=F32)
                + jnp.dot(tri, lo, preferred_element_type=F32))
        gcum_t = gcum.T
        gtot = jnp.sum(gates, axis=0, keepdims=True)
        for h in range(GDN_HEADS):
            beta = gates[:, h:h + 1]
            gl = GDN_HEADS + h
            gc = gcum[:, gl:gl + 1]
            gr = gcum_t[gl:gl + 1, :]
            g_last = gtot[:, gl:gl + 1]
            decay = jnp.where(causal, jnp.exp(jnp.where(causal, gc - gr, 0.0)), 0.0)
            kh = k_ref[0, rows, h * GDN_DK:(h + 1) * GDN_DK]
            qh = q_ref[0, rows, h * GDN_DK:(h + 1) * GDN_DK]
            vh = v_ref[0, rows, h * GDN_DV:(h + 1) * GDN_DV]
            kf = kh.astype(F32)
            kbeta = kf * beta
            nt = (((1,), (1,)), ((), ()))
            a_kk = lax.dot_general(kbeta.astype(BF16), kh, nt, preferred_element_type=F32)
            a_kk = jnp.where(strict, a_kk * decay, 0.0)
            a_qk = lax.dot_general(qh, kh, nt, preferred_element_type=F32) * decay
            tinv = _unit_triangular_inverse(a_kk).astype(BF16)
            eg = jnp.exp(gc)
            w = jnp.dot(tinv, (kbeta * eg).astype(BF16), preferred_element_type=F32)
            u = jnp.dot(tinv, (vh.astype(F32) * beta).astype(BF16), preferred_element_type=F32)
            state = s_ref[h]
            sb = state.astype(BF16)
            v_new = u - jnp.dot(w.astype(BF16), sb, preferred_element_type=F32)
            qg = (qh.astype(F32) * eg).astype(BF16)
            vnb = v_new.astype(BF16)
            o = (jnp.dot(qg, sb, preferred_element_type=F32)
                 + jnp.dot(a_qk.astype(BF16), vnb, preferred_element_type=F32))
            o_ref[0, 0, rows, h * GDN_DV:(h + 1) * GDN_DV] = o.astype(o_ref.dtype)
            kd = (kf * jnp.exp(g_last - gc)).astype(BF16)
            tn = (((0,), (0,)), ((), ()))
            s_ref[h] = state * jnp.exp(g_last) + lax.dot_general(kd, vnb, tn, preferred_element_type=F32)
        return carry

    lax.fori_loop(0, n_chunks, chunk_body, 0)


def _gated_delta(q, k, v, gates):
    b, t, _ = q.shape
    tb = GDN_BLOCK_TOKENS
    nblk = t // tb
    blk = lambda d, i: jnp.where(d == 0, i, nblk - 1 - i)
    tok = lambda width: pl.BlockSpec((1, tb, width), lambda d, bi, i: (bi, blk(d, i), 0))
    dir_tok = lambda width: pl.BlockSpec((1, 1, tb, width), lambda d, bi, i: (d, bi, blk(d, i), 0))
    return pl.pallas_call(
        _delta_kernel,
        grid=(2, b, nblk),
        in_specs=[tok(GDN_KW), tok(GDN_KW), tok(GDN_VW), dir_tok(GDN_GATE_LANES)],
        out_specs=dir_tok(GDN_VW),
        out_shape=jax.ShapeDtypeStruct((2, b, t, GDN_VW), BF16),
        scratch_shapes=[pltpu.VMEM((GDN_HEADS, GDN_DK, GDN_DV), F32)],
        compiler_params=_cparams(("parallel", "parallel", "arbitrary")),
        name="gated_delta",
    )(q, k, v, gates)


def _out_kernel(o_ref, z_ref, h_ref, nw_ref, wo_ref, fw_ref, y_ref, og_ref):
    for hd in range(GDN_HEADS):
        lanes = slice(hd * GDN_DV, (hd + 1) * GDN_DV)
        o = o_ref[0, :, lanes].astype(F32) + o_ref[1, :, lanes].astype(F32)
        z = z_ref[:, lanes].astype(F32)
        on = o * _rms_scale(o) * nw_ref[...]
        og_ref[:, lanes] = (on * _silu(z)).astype(og_ref.dtype)
    h = h_ref[...] + jnp.dot(og_ref[...], wo_ref[...], preferred_element_type=F32)
    y_ref[...] = h * _rms_scale(h) * fw_ref[...]


def _out_proj(o2, z2, h2, nw, wo, fw):
    n = h2.shape[0]
    tm = TOKEN_TILE
    tok = lambda width: pl.BlockSpec((tm, width), lambda i: (i, 0))
    return pl.pallas_call(
        _out_kernel,
        grid=(n // tm,),
        in_specs=[pl.BlockSpec((2, tm, GDN_VW), lambda i: (0, i, 0)), tok(GDN_VW), tok(D_MODEL),
                  _const_spec((1, GDN_DV)), _const_spec(wo.shape), _const_spec((1, D_MODEL))],
        out_specs=tok(D_MODEL),
        out_shape=jax.ShapeDtypeStruct((n, D_MODEL), F32),
        scratch_shapes=[pltpu.VMEM((tm, GDN_VW), BF16)],
        compiler_params=_cparams(("parallel",)),
        name="out_proj",
    )(o2, z2, h2, nw, wo, fw)


def _trunk(x, p):
    b, t, _ = x.shape
    n = b * t
    x2 = x.reshape(n, D_MODEL)
    q, k, v, g = _na_in_proj(x2, p["ln0"], p["na_w_in"])
    seq = lambda a: a.reshape(b, t, a.shape[-1])
    og = _neighbourhood_attention(seq(q), seq(k), seq(v), seq(g), p["na_bias"])
    h1, qkv, z, graw = _mid_proj(x2, og.reshape(n, NA_WIDTH), p["na_w_out"], p["ln1"],
                                 p["w_qkv"], p["w_z"], p["w_gate"])
    qn, kn, vn, gates = _conv_gates(seq(qkv), seq(graw), p["conv_w"], p["alog_l"], p["dtb_l"])
    o2 = _gated_delta(qn, kn, vn, gates)
    y = _out_proj(o2.reshape(2, n, GDN_VW), z, h1, p["gdn_norm_w"], p["gdn_w_out"], p["final_w"])
    return y.reshape(b, t, D_MODEL)


def kernel(x_prompt, x_sample, ln_w, na_w_in, na_rpb, na_w_out, gdn_w_in, gdn_conv_w, gdn_a_log,
           gdn_dt_bias, gdn_norm_w, gdn_w_out, final_norm_w):
    o1 = GDN_CONV_CH
    o2 = o1 + GDN_VW
    w_in = gdn_w_in[0]
    n_gate = 4 * GDN_HEADS
    lane_pad = lambda a: jnp.pad(a, ((0, 0), (0, GDN_GATE_LANES - a.shape[1])))
    shift = lambda a: jnp.pad(a.reshape(1, 2 * GDN_HEADS).astype(F32),
                              ((0, 0), (2 * GDN_HEADS, GDN_GATE_LANES - n_gate)))
    p = {
        "ln0": ln_w[0].reshape(1, D_MODEL),
        "ln1": ln_w[1].reshape(1, D_MODEL),
        "na_w_in": na_w_in[0].astype(BF16),
        "na_w_out": na_w_out[0].astype(BF16),
        "w_qkv": w_in[:, :o1].astype(BF16),
        "w_z": w_in[:, o1:o2].astype(BF16),
        "w_gate": lane_pad(w_in[:, o2:]).astype(BF16),
        "conv_w": jnp.pad(gdn_conv_w[0], ((0, 8 - GDN_CONV), (0, 0))),
        "alog_l": shift(gdn_a_log[0]),
        "dtb_l": shift(gdn_dt_bias[0]),
        "gdn_norm_w": gdn_norm_w[0].reshape(1, GDN_DV),
        "gdn_w_out": gdn_w_out[0].astype(BF16),
        "final_w": final_norm_w.reshape(1, D_MODEL),
    }
    outs = []
    for x in (x_prompt, x_sample):
        p["na_bias"] = _na_bias_tables(na_rpb[0], x.shape[1] // GRID_W)
        outs.append(_trunk(x, p))
    return tuple(outs)
```

```python
import functools

import numpy as np
import jax
import jax.numpy as jnp
from jax import lax
from jax.experimental import pallas as pl
from jax.experimental.pallas import tpu as pltpu

F32 = jnp.float32
BF16 = jnp.bfloat16

D_MODEL = 1024
GRID_W = 64
RMS_EPS = 1e-6
L2_EPS = 1e-6
NEG_INF = -1e30

NA_HEADS = 16
NA_HEAD_DIM = 64
NA_WIDTH = NA_HEADS * NA_HEAD_DIM
NA_WIN_H = 8
NA_WIN_W = 16
NA_ROWS_PER_BLOCK = 4
NA_WIN_TOKENS = NA_WIN_H * GRID_W

GDN_HEADS = 8
GDN_DK = 128
GDN_DV = 256
GDN_KW = GDN_HEADS * GDN_DK
GDN_VW = GDN_HEADS * GDN_DV
GDN_CONV = 5
GDN_CONV_CH = 2 * GDN_KW + GDN_VW
GDN_CHUNK = 64
GDN_GATE_LANES = 128
GDN_BLOCK_TOKENS = 256
CONV_HALO = 16

TOKEN_TILE = 512
VMEM_LIMIT = 56 * 1024 * 1024


def _cparams(sem):
    return pltpu.CompilerParams(dimension_semantics=sem, vmem_limit_bytes=VMEM_LIMIT)


def _rms_scale(x):
    return lax.rsqrt(jnp.mean(x * x, axis=-1, keepdims=True) + RMS_EPS)


def _silu(x):
    return x * jax.nn.sigmoid(x)


def _const_spec(shape):
    nd = len(shape)
    return pl.BlockSpec(shape, lambda *_: (0,) * nd, pipeline_mode=pl.Buffered(1))


def _na_in_kernel(x_ref, lnw_ref, w_ref, q_ref, k_ref, v_ref, g_ref):
    x = x_ref[...]
    xn = (x * _rms_scale(x) * lnw_ref[...]).astype(BF16)
    outs = (q_ref, k_ref, v_ref, g_ref)
    for j, o_ref in enumerate(outs):
        y = jnp.dot(xn, w_ref[:, j * NA_WIDTH:(j + 1) * NA_WIDTH], preferred_element_type=F32)
        if j == 0:
            y = y * (NA_HEAD_DIM ** -0.5)
        o_ref[...] = y.astype(o_ref.dtype)


def _na_in_proj(x2, lnw, w):
    n = x2.shape[0]
    tm = TOKEN_TILE
    tok = lambda width: pl.BlockSpec((tm, width), lambda i: (i, 0))
    out = jax.ShapeDtypeStruct((n, NA_WIDTH), BF16)
    return pl.pallas_call(
        _na_in_kernel,
        grid=(n // tm,),
        in_specs=[tok(D_MODEL), _const_spec((1, D_MODEL)), _const_spec((D_MODEL, 4 * NA_WIDTH))],
        out_specs=[tok(NA_WIDTH)] * 4,
        out_shape=[out] * 4,
        compiler_params=_cparams(("parallel",)),
        name="na_in_proj",
    )(x2, lnw, w)


def _na_bias_tables(rpb, rows):
    qc = np.arange(GRID_W)[:, None]
    kc = np.arange(GRID_W)[None, :]
    win_start = np.clip(qc - NA_WIN_W // 2, 0, GRID_W - NA_WIN_W)
    col_ok = (kc >= win_start) & (kc < win_start + NA_WIN_W)
    dc_idx = np.clip(kc - qc + NA_WIN_W - 1, 0, 2 * NA_WIN_W - 2)
    rb = NA_ROWS_PER_BLOCK
    d_first = np.arange(rb)
    d_mid = np.full((rb,), NA_WIN_H // 2)
    d_last = NA_WIN_H - rb + np.arange(rb)
    d_all = np.concatenate([d_first, d_mid, d_last])
    dr = np.arange(NA_WIN_H)[None, :] - d_all[:, None] + (NA_WIN_H - 1)
    b = rpb.astype(F32)[:, dr]
    b = b[:, :, :, dc_idx]
    b = jnp.where(col_ok[None, None, None], b, NEG_INF)
    b = b.transpose(1, 0, 3, 2, 4)
    return b.reshape(3 * rb, NA_HEADS, GRID_W, NA_WIN_TOKENS)


def _na_kernel(q_ref, g_ref, kp_ref, kc_ref, kn_ref, vp_ref, vc_ref, vn_ref, bias_ref,
               o_ref, kcat_ref, vcat_ref, *, rows):
    rb = NA_ROWS_PER_BLOCK
    blk_tokens = rb * GRID_W
    i = pl.program_id(0)
    kcat_ref[0:blk_tokens, :] = kp_ref[0]
    kcat_ref[blk_tokens:2 * blk_tokens, :] = kc_ref[0]
    kcat_ref[2 * blk_tokens:3 * blk_tokens, :] = kn_ref[0]
    vcat_ref[0:blk_tokens, :] = vp_ref[0]
    vcat_ref[blk_tokens:2 * blk_tokens, :] = vc_ref[0]
    vcat_ref[2 * blk_tokens:3 * blk_tokens, :] = vn_ref[0]

    lane = lax.broadcasted_iota(jnp.int32, (GRID_W, 2 * NA_HEAD_DIM), 1)
    first_half = lane < NA_HEAD_DIM
    r0 = i * rb
    for qi in range(rb):
        r = r0 + qi
        rs = jnp.clip(r - NA_WIN_H // 2, 0, rows - NA_WIN_H)
        off = pl.multiple_of((rs - (r0 - rb)) * GRID_W, GRID_W)
        qrows = pl.ds(qi * GRID_W, GRID_W)
        krows = pl.ds(off, NA_WIN_TOKENS)
        nt = (((1,), (1,)), ((), ()))
        pair_lanes = [pl.ds(p * 2 * NA_HEAD_DIM, 2 * NA_HEAD_DIM) for p in range(NA_HEADS // 2)]
        ss = []
        for p, lanes in enumerate(pair_lanes):
            qp = q_ref[0, qrows, lanes]
            kw = kcat_ref[krows, lanes]
            zero = jnp.zeros_like(qp)
            for half in range(2):
                qh = jnp.where(first_half, qp, zero) if half == 0 else jnp.where(first_half, zero, qp)
                s = lax.dot_general(qh, kw, nt, preferred_element_type=F32)
                ss.append(s + bias_ref[qi, 2 * p + half])
        es, ls = [], []
        for s in ss:
            e = jnp.exp(s - jnp.max(s, axis=-1, keepdims=True))
            ls.append(jnp.sum(e, axis=-1, keepdims=True))
            es.append(e.astype(BF16))
        for p, lanes in enumerate(pair_lanes):
            vw = vcat_ref[krows, lanes]
            o_half = [jnp.dot(es[2 * p + half], vw, preferred_element_type=F32) / ls[2 * p + half]
                      for half in range(2)]
            o = jnp.where(first_half, o_half[0], o_half[1])
            g = g_ref[0, qrows, lanes].astype(F32)
            o_ref[0, qrows, lanes] = (o * _silu(g)).astype(o_ref.dtype)


def _neighbourhood_attention(q, k, v, g, bias):
    b, t, _ = q.shape
    rows = t // GRID_W
    rb = NA_ROWS_PER_BLOCK
    assert rows % rb == 0 and rows >= 2 * NA_WIN_H
    nb = rows // rb
    bt = rb * GRID_W
    cur = pl.BlockSpec((1, bt, NA_WIDTH), lambda i, bi: (bi, i, 0))
    prev = pl.BlockSpec((1, bt, NA_WIDTH), lambda i, bi: (bi, jnp.maximum(i - 1, 0), 0))
    nxt = pl.BlockSpec((1, bt, NA_WIDTH), lambda i, bi: (bi, jnp.minimum(i + 1, nb - 1), 0))
    case = lambda i: jnp.where(i == 0, 0, jnp.where(i == nb - 1, 2, 1))
    bias_spec = pl.BlockSpec((rb, NA_HEADS, GRID_W, NA_WIN_TOKENS), lambda i, bi: (case(i), 0, 0, 0))
    return pl.pallas_call(
        functools.partial(_na_kernel, rows=rows),
        grid=(nb, b),
        in_specs=[cur, cur, prev, cur, nxt, prev, cur, nxt, bias_spec],
        out_specs=cur,
        out_shape=jax.ShapeDtypeStruct((b, t, NA_WIDTH), BF16),
        scratch_shapes=[pltpu.VMEM((3 * bt, NA_WIDTH), BF16), pltpu.VMEM((3 * bt, NA_WIDTH), BF16)],
        compiler_params=_cparams(("arbitrary", "arbitrary")),
        name="neighbourhood_attention",
    )(q, g, k, k, k, v, v, v, bias)


def _mid_kernel(x_ref, og_ref, wo_ref, lnw_ref, wqkv_ref, wz_ref, wg_ref,
                h_ref, qkv_ref, z_ref, gate_ref):
    h = x_ref[...] + jnp.dot(og_ref[...], wo_ref[...], preferred_element_type=F32)
    h_ref[...] = h
    hn = (h * _rms_scale(h) * lnw_ref[...]).astype(BF16)
    qkv_ref[...] = jnp.dot(hn, wqkv_ref[...], preferred_element_type=F32).astype(qkv_ref.dtype)
    z_ref[...] = jnp.dot(hn, wz_ref[...], preferred_element_type=F32).astype(z_ref.dtype)
    gate_ref[...] = jnp.dot(hn, wg_ref[...], preferred_element_type=F32)


def _mid_proj(x2, og2, wo, lnw, wqkv, wz, wg):
    n = x2.shape[0]
    tm = TOKEN_TILE // 2
    tok = lambda width: pl.BlockSpec((tm, width), lambda i: (i, 0))
    return pl.pallas_call(
        _mid_kernel,
        grid=(n // tm,),
        in_specs=[tok(D_MODEL), tok(NA_WIDTH), _const_spec(wo.shape), _const_spec((1, D_MODEL)),
                  _const_spec(wqkv.shape), _const_spec(wz.shape), _const_spec(wg.shape)],
        out_specs=[tok(D_MODEL), tok(GDN_CONV_CH), tok(GDN_VW), tok(GDN_GATE_LANES)],
        out_shape=[jax.ShapeDtypeStruct((n, D_MODEL), F32),
                   jax.ShapeDtypeStruct((n, GDN_CONV_CH), BF16),
                   jax.ShapeDtypeStruct((n, GDN_VW), BF16),
                   jax.ShapeDtypeStruct((n, GDN_GATE_LANES), F32)],
        compiler_params=_cparams(("parallel",)),
        name="mid_proj",
    )(x2, og2, wo, lnw, wqkv, wz, wg)


def _conv_kernel(prev_ref, cur_ref, next_ref, cw_ref, graw_ref, alog_ref, dtb_ref,
                 q_ref, k_ref, v_ref, gate_ref, ext_ref):
    i = pl.program_id(1)
    nblk = pl.num_programs(1)
    tb = GDN_BLOCK_TOKENS
    pad = GDN_CONV // 2
    halo = 8
    before = prev_ref[0].astype(F32)[CONV_HALO - halo:, :]
    after = next_ref[0].astype(F32)[:halo, :]
    ext_ref[0:halo, :] = jnp.where(i > 0, before, 0.0)
    ext_ref[halo:halo + tb, :] = cur_ref[0].astype(F32)
    ext_ref[halo + tb:, :] = jnp.where(i < nblk - 1, after, 0.0)

    def conv_silu(lo, hi):
        acc = None
        for j in range(GDN_CONV):
            term = ext_ref[halo - pad + j:halo - pad + j + tb, lo:hi] * cw_ref[j:j + 1, lo:hi]
            acc = term if acc is None else acc + term
        return _silu(acc)

    for h in range(GDN_HEADS):
        for base, o_ref, scale in ((0, q_ref, GDN_DK ** -0.5), (GDN_KW, k_ref, 1.0)):
            y = conv_silu(base + h * GDN_DK, base + (h + 1) * GDN_DK)
            y = y * (lax.rsqrt(jnp.sum(y * y, axis=-1, keepdims=True) + L2_EPS) * scale)
            o_ref[0, :, h * GDN_DK:(h + 1) * GDN_DK] = y.astype(o_ref.dtype)
    for c in range(GDN_VW // 256):
        y = conv_silu(2 * GDN_KW + c * 256, 2 * GDN_KW + (c + 1) * 256)
        v_ref[0, :, c * 256:(c + 1) * 256] = y.astype(v_ref.dtype)

    x = graw_ref[0]
    lane = lax.broadcasted_iota(jnp.int32, x.shape, 1)
    beta = jax.nn.sigmoid(x)
    decay = -jnp.exp(alog_ref[...]) * jax.nn.softplus(x + dtb_ref[...])
    for d in range(2):
        bsh = beta if d == 0 else pltpu.roll(beta, GDN_GATE_LANES - d * GDN_HEADS, 1)
        gsh = pltpu.roll(decay, GDN_GATE_LANES - (1 + d) * GDN_HEADS, 1)
        gate_ref[d, 0] = jnp.where(lane < GDN_HEADS, bsh, jnp.where(lane < 2 * GDN_HEADS, gsh, 0.0))


def _conv_gates(qkv, graw, conv_w, alog_l, dtb_l):
    b, t, _ = qkv.shape
    tb = GDN_BLOCK_TOKENS
    nblk = t // tb
    hb = tb // CONV_HALO
    nh = t // CONV_HALO
    cur = lambda width: pl.BlockSpec((1, tb, width), lambda bi, i: (bi, i, 0))
    prev = pl.BlockSpec((1, CONV_HALO, GDN_CONV_CH), lambda bi, i: (bi, jnp.maximum(i * hb - 1, 0), 0))
    nxt = pl.BlockSpec((1, CONV_HALO, GDN_CONV_CH),
                       lambda bi, i: (bi, jnp.minimum((i + 1) * hb, nh - 1), 0))
    return pl.pallas_call(
        _conv_kernel,
        grid=(b, nblk),
        in_specs=[prev, cur(GDN_CONV_CH), nxt, _const_spec(conv_w.shape), cur(GDN_GATE_LANES),
                  _const_spec((1, GDN_GATE_LANES)), _const_spec((1, GDN_GATE_LANES))],
        out_specs=[cur(GDN_KW), cur(GDN_KW), cur(GDN_VW),
                   pl.BlockSpec((2, 1, tb, GDN_GATE_LANES), lambda bi, i: (0, bi, i, 0))],
        out_shape=[jax.ShapeDtypeStruct((b, t, GDN_KW), BF16),
                   jax.ShapeDtypeStruct((b, t, GDN_KW), BF16),
                   jax.ShapeDtypeStruct((b, t, GDN_VW), BF16),
                   jax.ShapeDtypeStruct((2, b, t, GDN_GATE_LANES), F32)],
        scratch_shapes=[pltpu.VMEM((tb + 16, GDN_CONV_CH), F32)],
        compiler_params=_cparams(("parallel", "arbitrary")),
        name="conv_gates",
    )(qkv, qkv, qkv, conv_w, graw, alog_l, dtb_l)


def _split3(x):
    hi = x.astype(BF16)
    r1 = x - hi.astype(F32)
    mid = r1.astype(BF16)
    lo = (r1 - mid.astype(F32)).astype(BF16)
    return hi, mid, lo


def _dot(a, b):
    return jnp.dot(a, b, preferred_element_type=F32)


def _unit_triangular_inverses(a_list):
    c = a_list[0].shape[0]
    eye = (lax.broadcasted_iota(jnp.int32, (c, c), 0) == lax.broadcasted_iota(jnp.int32, (c, c), 1))
    xs = [jnp.where(eye, 1.0, 0.0) - a for a in a_list]
    abs_ = [a.astype(BF16) for a in a_list]
    ps = [_dot(ab, ab) for ab in abs_]
    for _ in range(int(np.log2(c)) - 2):
        pbs = [p.astype(BF16) for p in ps]
        xps = [_dot(jnp.concatenate([x.astype(BF16), pb], axis=0), pb) for x, pb in zip(xs, pbs)]
        xs = [x + xp[:c] for x, xp in zip(xs, xps)]
        ps = [xp[c:] for xp in xps]
    return [x + _dot(x.astype(BF16), p.astype(BF16)) for x, p in zip(xs, ps)]


def _delta_prepare(cc, ins, wq_ref, u_ref, aqk_ref, kd_ref, glast_ref):
    c = GDN_CHUNK
    rows = pl.ds(cc * c, c)
    row = lax.broadcasted_iota(jnp.int32, (c, c), 0)
    col = lax.broadcasted_iota(jnp.int32, (c, c), 1)
    nt = (((1,), (1,)), ((), ()))
    chains = []
    for direction, (q_ref, k_ref, v_ref, gate_ref) in enumerate(ins):
        causal = (row >= col) if direction == 0 else (row <= col)
        strict = (row > col) if direction == 0 else (row < col)
        tri = jnp.where(causal, 1.0, 0.0).astype(BF16)
        gates = gate_ref[0, 0, rows, :]
        hi, mid, lo = _split3(gates)
        gcum = _dot(tri, hi) + _dot(tri, mid) + _dot(tri, lo)
        gcum_t = gcum.T
        gtot = jnp.sum(gates, axis=0, keepdims=True)
        glast_ref[cc, direction] = jnp.broadcast_to(jnp.exp(gtot), (8, GDN_GATE_LANES))
        for h in range(GDN_HEADS):
            gl = GDN_HEADS + h
            chains.append(dict(
                causal=causal, strict=strict,
                beta=gates[:, h:h + 1], gc=gcum[:, gl:gl + 1], gr=gcum_t[gl:gl + 1, :],
                g_last=gtot[:, gl:gl + 1],
                k=k_ref[0, rows, h * GDN_DK:(h + 1) * GDN_DK],
                q=q_ref[0, rows, h * GDN_DK:(h + 1) * GDN_DK],
                v=v_ref[0, rows, h * GDN_DV:(h + 1) * GDN_DV]))
    for ch in chains:
        ch["kf"] = ch["k"].astype(F32)
        ch["kbeta"] = ch["kf"] * ch["beta"]
    kqs = [lax.dot_general(jnp.concatenate([ch["kbeta"].astype(BF16), ch["q"]], axis=0), ch["k"], nt,
                           preferred_element_type=F32) for ch in chains]
    a_list = []
    for ch, kq in zip(chains, kqs):
        decay = jnp.where(ch["causal"], jnp.exp(jnp.where(ch["causal"], ch["gc"] - ch["gr"], 0.0)), 0.0)
        a_list.append(jnp.where(ch["strict"], kq[:c] * decay, 0.0))
        ch["aqk"] = (kq[c:] * decay).astype(BF16)
    tinvs = _unit_triangular_inverses(a_list)
    rhss = []
    for ch in chains:
        ch["eg"] = jnp.exp(ch["gc"])
        rhss.append(jnp.concatenate([(ch["kbeta"] * ch["eg"]).astype(BF16),
                                     (ch["v"].astype(F32) * ch["beta"]).astype(BF16)], axis=1))
    wus = [_dot(t.astype(BF16), rhs) for t, rhs in zip(tinvs, rhss)]
    for g, (ch, wu) in enumerate(zip(chains, wus)):
        wq_ref[cc, g, 0:c, :] = wu[:, :GDN_DK].astype(BF16)
        wq_ref[cc, g, c:2 * c, :] = (ch["q"].astype(F32) * ch["eg"]).astype(BF16)
        u_ref[cc, g] = wu[:, GDN_DK:]
        aqk_ref[cc, g] = ch["aqk"]
        kd_ref[cc, g] = (ch["kf"] * jnp.exp(ch["g_last"] - ch["gc"])).astype(BF16)


def _delta_kernel(qf_ref, kf_ref, vf_ref, gf_ref, qb_ref, kb_ref, vb_ref, gb_ref,
                  of_ref, ob_ref, s_ref, wq_ref, u_ref, aqk_ref, kd_ref, glast_ref):
    i = pl.program_id(1)
    c = GDN_CHUNK
    n_chunks = GDN_BLOCK_TOKENS // c
    ins = ((qf_ref, kf_ref, vf_ref, gf_ref), (qb_ref, kb_ref, vb_ref, gb_ref))
    outs = (of_ref, ob_ref)

    @pl.when(i == 0)
    def _():
        s_ref[...] = jnp.zeros_like(s_ref)

    for cc in range(n_chunks):
        _delta_prepare(cc, ins, wq_ref, u_ref, aqk_ref, kd_ref, glast_ref)

    tn = (((0,), (0,)), ((), ()))
    n_chains = 2 * GDN_HEADS
    for ci in range(n_chunks):
        chunk_of = [ci if g < GDN_HEADS else n_chunks - 1 - ci for g in range(n_chains)]
        wss = [_dot(wq_ref[chunk_of[g], g], s_ref[g].astype(BF16)) for g in range(n_chains)]
        vnbs = [(u_ref[chunk_of[g], g] - wss[g][:c]).astype(BF16) for g in range(n_chains)]
        for g in range(n_chains):
            o = wss[g][c:] + _dot(aqk_ref[chunk_of[g], g], vnbs[g])
            h = g % GDN_HEADS
            outs[g // GDN_HEADS][0, pl.ds(chunk_of[g] * c, c), h * GDN_DV:(h + 1) * GDN_DV] = o.astype(BF16)
        for g in range(n_chains):
            gl = GDN_HEADS + g % GDN_HEADS
            e_last = glast_ref[chunk_of[g], g // GDN_HEADS, 0:1, gl:gl + 1]
            s_ref[g] = s_ref[g] * e_last + lax.dot_general(kd_ref[chunk_of[g], g], vnbs[g], tn,
                                                           preferred_element_type=F32)


def _gated_delta(q, k, v, gates):
    b, t, _ = q.shape
    tb = GDN_BLOCK_TOKENS
    nblk = t // tb
    n_chunks = tb // GDN_CHUNK
    n_chains = 2 * GDN_HEADS
    fwd = lambda width: pl.BlockSpec((1, tb, width), lambda bi, i: (bi, i, 0))
    bwd = lambda width: pl.BlockSpec((1, tb, width), lambda bi, i: (bi, nblk - 1 - i, 0))
    gate_f = pl.BlockSpec((1, 1, tb, GDN_GATE_LANES), lambda bi, i: (0, bi, i, 0))
    gate_b = pl.BlockSpec((1, 1, tb, GDN_GATE_LANES), lambda bi, i: (1, bi, nblk - 1 - i, 0))
    out = jax.ShapeDtypeStruct((b, t, GDN_VW), BF16)
    return pl.pallas_call(
        _delta_kernel,
        grid=(b, nblk),
        in_specs=[fwd(GDN_KW), fwd(GDN_KW), fwd(GDN_VW), gate_f,
                  bwd(GDN_KW), bwd(GDN_KW), bwd(GDN_VW), gate_b],
        out_specs=[fwd(GDN_VW), bwd(GDN_VW)],
        out_shape=[out, out],
        scratch_shapes=[pltpu.VMEM((n_chains, GDN_DK, GDN_DV), F32),
                        pltpu.VMEM((n_chunks, n_chains, 2 * GDN_CHUNK, GDN_DK), BF16),
                        pltpu.VMEM((n_chunks, n_chains, GDN_CHUNK, GDN_DV), F32),
                        pltpu.VMEM((n_chunks, n_chains, GDN_CHUNK, GDN_CHUNK), BF16),
                        pltpu.VMEM((n_chunks, n_chains, GDN_CHUNK, GDN_DK), BF16),
                        pltpu.VMEM((n_chunks, 2, 8, GDN_GATE_LANES), F32)],
        compiler_params=_cparams(("parallel", "arbitrary")),
        name="gated_delta",
    )(q, k, v, gates, q, k, v, gates)


def _out_kernel(of_ref, ob_ref, z_ref, h_ref, nw_ref, wo_ref, fw_ref, y_ref, og_ref):
    for hd in range(GDN_HEADS):
        lanes = slice(hd * GDN_DV, (hd + 1) * GDN_DV)
        o = of_ref[:, lanes].astype(F32) + ob_ref[:, lanes].astype(F32)
        z = z_ref[:, lanes].astype(F32)
        on = o * _rms_scale(o) * nw_ref[...]
        og_ref[:, lanes] = (on * _silu(z)).astype(og_ref.dtype)
    h = h_ref[...] + jnp.dot(og_ref[...], wo_ref[...], preferred_element_type=F32)
    y_ref[...] = h * _rms_scale(h) * fw_ref[...]


def _out_proj(of2, ob2, z2, h2, nw, wo, fw):
    n = h2.shape[0]
    tm = TOKEN_TILE
    tok = lambda width: pl.BlockSpec((tm, width), lambda i: (i, 0))
    return pl.pallas_call(
        _out_kernel,
        grid=(n // tm,),
        in_specs=[tok(GDN_VW), tok(GDN_VW), tok(GDN_VW), tok(D_MODEL),
                  _const_spec((1, GDN_DV)), _const_spec(wo.shape), _const_spec((1, D_MODEL))],
        out_specs=tok(D_MODEL),
        out_shape=jax.ShapeDtypeStruct((n, D_MODEL), F32),
        scratch_shapes=[pltpu.VMEM((tm, GDN_VW), BF16)],
        compiler_params=_cparams(("parallel",)),
        name="out_proj",
    )(of2, ob2, z2, h2, nw, wo, fw)


def _trunk(x, p):
    b, t, _ = x.shape
    n = b * t
    x2 = x.reshape(n, D_MODEL)
    q, k, v, g = _na_in_proj(x2, p["ln0"], p["na_w_in"])
    seq = lambda a: a.reshape(b, t, a.shape[-1])
    og = _neighbourhood_attention(seq(q), seq(k), seq(v), seq(g), p["na_bias"])
    h1, qkv, z, graw = _mid_proj(x2, og.reshape(n, NA_WIDTH), p["na_w_out"], p["ln1"],
                                 p["w_qkv"], p["w_z"], p["w_gate"])
    qn, kn, vn, gates = _conv_gates(seq(qkv), seq(graw), p["conv_w"], p["alog_l"], p["dtb_l"])
    o_f, o_b = _gated_delta(qn, kn, vn, gates)
    y = _out_proj(o_f.reshape(n, GDN_VW), o_b.reshape(n, GDN_VW), z, h1,
                  p["gdn_norm_w"], p["gdn_w_out"], p["final_w"])
    return y.reshape(b, t, D_MODEL)


def kernel(x_prompt, x_sample, ln_w, na_w_in, na_rpb, na_w_out, gdn_w_in, gdn_conv_w, gdn_a_log,
           gdn_dt_bias, gdn_norm_w, gdn_w_out, final_norm_w):
    o1 = GDN_CONV_CH
    o2 = o1 + GDN_VW
    w_in = gdn_w_in[0]
    n_gate = 4 * GDN_HEADS
    lane_pad = lambda a: jnp.pad(a, ((0, 0), (0, GDN_GATE_LANES - a.shape[1])))
    shift = lambda a: jnp.pad(a.reshape(1, 2 * GDN_HEADS).astype(F32),
                              ((0, 0), (2 * GDN_HEADS, GDN_GATE_LANES - n_gate)))
    p = {
        "ln0": ln_w[0].reshape(1, D_MODEL),
        "ln1": ln_w[1].reshape(1, D_MODEL),
        "na_w_in": na_w_in[0].astype(BF16),
        "na_w_out": na_w_out[0].astype(BF16),
        "w_qkv": w_in[:, :o1].astype(BF16),
        "w_z": w_in[:, o1:o2].astype(BF16),
        "w_gate": lane_pad(w_in[:, o2:]).astype(BF16),
        "conv_w": jnp.pad(gdn_conv_w[0], ((0, 8 - GDN_CONV), (0, 0))),
        "alog_l": shift(gdn_a_log[0]),
        "dtb_l": shift(gdn_dt_bias[0]),
        "gdn_norm_w": gdn_norm_w[0].reshape(1, GDN_DV),
        "gdn_w_out": gdn_w_out[0].astype(BF16),
        "final_w": final_norm_w.reshape(1, D_MODEL),
    }
    outs = []
    for x in (x_prompt, x_sample):
        p["na_bias"] = _na_bias_tables(na_rpb[0], x.shape[1] // GRID_W)
        outs.append(_trunk(x, p))
    return tuple(outs)
```

```python
import functools

import numpy as np
import jax
import jax.numpy as jnp
from jax import lax
from jax.experimental import pallas as pl
from jax.experimental.pallas import tpu as pltpu

F32 = jnp.float32
BF16 = jnp.bfloat16

D_MODEL = 1024
GRID_W = 64
RMS_EPS = 1e-6
L2_EPS = 1e-6
NEG_INF = -1e30

NA_HEADS = 16
NA_HEAD_DIM = 64
NA_WIDTH = NA_HEADS * NA_HEAD_DIM
NA_WIN_H = 8
NA_WIN_W = 16
NA_ROWS_PER_BLOCK = 4
NA_WIN_TOKENS = NA_WIN_H * GRID_W

GDN_HEADS = 8
GDN_DK = 128
GDN_DV = 256
GDN_KW = GDN_HEADS * GDN_DK
GDN_VW = GDN_HEADS * GDN_DV
GDN_CONV = 5
GDN_CONV_CH = 2 * GDN_KW + GDN_VW
GDN_CHUNK = 64
GDN_GATE_LANES = 128
GDN_BLOCK_TOKENS = 256
CONV_HALO = 16
CONV_GROUP_ROWS = 64
CONV_LANE_CHUNK = 512

TOKEN_TILE = 512
VMEM_LIMIT = 56 * 1024 * 1024


def _cparams(sem):
    return pltpu.CompilerParams(dimension_semantics=sem, vmem_limit_bytes=VMEM_LIMIT)


def _rms_scale(x):
    return lax.rsqrt(jnp.mean(x * x, axis=-1, keepdims=True) + RMS_EPS)


def _silu(x):
    return x * jax.nn.sigmoid(x)


def _const_spec(shape):
    nd = len(shape)
    return pl.BlockSpec(shape, lambda *_: (0,) * nd, pipeline_mode=pl.Buffered(1))


def _na_in_kernel(x_ref, lnw_ref, w_ref, q_ref, k_ref, v_ref, g_ref):
    x = x_ref[...]
    xn = (x * _rms_scale(x) * lnw_ref[...]).astype(BF16)
    outs = (q_ref, k_ref, v_ref, g_ref)
    for j, o_ref in enumerate(outs):
        y = jnp.dot(xn, w_ref[:, j * NA_WIDTH:(j + 1) * NA_WIDTH], preferred_element_type=F32)
        if j == 0:
            y = y * (NA_HEAD_DIM ** -0.5)
        o_ref[...] = y.astype(o_ref.dtype)


def _na_in_proj(x2, lnw, w):
    n = x2.shape[0]
    tm = TOKEN_TILE
    tok = lambda width: pl.BlockSpec((tm, width), lambda i: (i, 0))
    out = jax.ShapeDtypeStruct((n, NA_WIDTH), BF16)
    return pl.pallas_call(
        _na_in_kernel,
        grid=(n // tm,),
        in_specs=[tok(D_MODEL), _const_spec((1, D_MODEL)), _const_spec((D_MODEL, 4 * NA_WIDTH))],
        out_specs=[tok(NA_WIDTH)] * 4,
        out_shape=[out] * 4,
        compiler_params=_cparams(("parallel",)),
        name="na_in_proj",
    )(x2, lnw, w)


def _na_bias_tables(rpb):
    qc = np.arange(GRID_W)[:, None]
    kc = np.arange(GRID_W)[None, :]
    win_start = np.clip(qc - NA_WIN_W // 2, 0, GRID_W - NA_WIN_W)
    col_ok = (kc >= win_start) & (kc < win_start + NA_WIN_W)
    side = GRID_W - NA_WIN_W
    padded = jnp.pad(rpb.astype(F32), ((0, 0), (0, 0), (side, side)))
    toeplitz = jnp.stack([padded[:, :, GRID_W - 1 - c:2 * GRID_W - 1 - c] for c in range(GRID_W)],
                         axis=2)
    toeplitz = jnp.where(col_ok[None, None], toeplitz, NEG_INF)
    rb = NA_ROWS_PER_BLOCK
    d_all = list(range(rb)) + [NA_WIN_H // 2] * rb + [NA_WIN_H - rb + j for j in range(rb)]
    tables = []
    for d in d_all:
        t = toeplitz[:, NA_WIN_H - 1 - d:2 * NA_WIN_H - 1 - d]
        tables.append(t.transpose(0, 2, 1, 3).reshape(NA_HEADS, GRID_W, NA_WIN_TOKENS))
    return jnp.stack(tables)


def _na_kernel(q_ref, g_ref, kp_ref, kc_ref, kn_ref, vp_ref, vc_ref, vn_ref, bias_ref,
               o_ref, kcat_ref, vcat_ref, *, rows):
    rb = NA_ROWS_PER_BLOCK
    blk_tokens = rb * GRID_W
    i = pl.program_id(0)
    kcat_ref[0:blk_tokens, :] = kp_ref[0]
    kcat_ref[blk_tokens:2 * blk_tokens, :] = kc_ref[0]
    kcat_ref[2 * blk_tokens:3 * blk_tokens, :] = kn_ref[0]
    vcat_ref[0:blk_tokens, :] = vp_ref[0]
    vcat_ref[blk_tokens:2 * blk_tokens, :] = vc_ref[0]
    vcat_ref[2 * blk_tokens:3 * blk_tokens, :] = vn_ref[0]

    lane = lax.broadcasted_iota(jnp.int32, (GRID_W, 2 * NA_HEAD_DIM), 1)
    first_half = lane < NA_HEAD_DIM
    r0 = i * rb
    for qi in range(rb):
        r = r0 + qi
        rs = jnp.clip(r - NA_WIN_H // 2, 0, rows - NA_WIN_H)
        off = pl.multiple_of((rs - (r0 - rb)) * GRID_W, GRID_W)
        qrows = pl.ds(qi * GRID_W, GRID_W)
        krows = pl.ds(off, NA_WIN_TOKENS)
        nt = (((1,), (1,)), ((), ()))
        pair_lanes = [pl.ds(p * 2 * NA_HEAD_DIM, 2 * NA_HEAD_DIM) for p in range(NA_HEADS // 2)]
        ss = []
        for p, lanes in enumerate(pair_lanes):
            qp = q_ref[0, qrows, lanes]
            kw = kcat_ref[krows, lanes]
            zero = jnp.zeros_like(qp)
            for half in range(2):
                qh = jnp.where(first_half, qp, zero) if half == 0 else jnp.where(first_half, zero, qp)
                s = lax.dot_general(qh, kw, nt, preferred_element_type=F32)
                ss.append(s + bias_ref[qi, 2 * p + half])
        es, ls = [], []
        for s in ss:
            e = jnp.exp(s - jnp.max(s, axis=-1, keepdims=True))
            ls.append(jnp.sum(e, axis=-1, keepdims=True))
            es.append(e.astype(BF16))
        for p, lanes in enumerate(pair_lanes):
            vw = vcat_ref[krows, lanes]
            o_half = [jnp.dot(es[2 * p + half], vw, preferred_element_type=F32) / ls[2 * p + half]
                      for half in range(2)]
            o = jnp.where(first_half, o_half[0], o_half[1])
            g = g_ref[0, qrows, lanes].astype(F32)
            o_ref[0, qrows, lanes] = (o * _silu(g)).astype(o_ref.dtype)


def _neighbourhood_attention(q, k, v, g, bias):
    b, t, _ = q.shape
    rows = t // GRID_W
    rb = NA_ROWS_PER_BLOCK
    assert rows % rb == 0 and rows >= 2 * NA_WIN_H
    nb = rows // rb
    bt = rb * GRID_W
    cur = pl.BlockSpec((1, bt, NA_WIDTH), lambda i, bi: (bi, i, 0))
    prev = pl.BlockSpec((1, bt, NA_WIDTH), lambda i, bi: (bi, jnp.maximum(i - 1, 0), 0))
    nxt = pl.BlockSpec((1, bt, NA_WIDTH), lambda i, bi: (bi, jnp.minimum(i + 1, nb - 1), 0))
    case = lambda i: jnp.where(i == 0, 0, jnp.where(i == nb - 1, 2, 1))
    bias_spec = pl.BlockSpec((rb, NA_HEADS, GRID_W, NA_WIN_TOKENS), lambda i, bi: (case(i), 0, 0, 0))
    return pl.pallas_call(
        functools.partial(_na_kernel, rows=rows),
        grid=(nb, b),
        in_specs=[cur, cur, prev, cur, nxt, prev, cur, nxt, bias_spec],
        out_specs=cur,
        out_shape=jax.ShapeDtypeStruct((b, t, NA_WIDTH), BF16),
        scratch_shapes=[pltpu.VMEM((3 * bt, NA_WIDTH), BF16), pltpu.VMEM((3 * bt, NA_WIDTH), BF16)],
        compiler_params=_cparams(("arbitrary", "arbitrary")),
        name="neighbourhood_attention",
    )(q, g, k, k, k, v, v, v, bias)


def _mid_kernel(x_ref, og_ref, wo_ref, lnw_ref, wqkv_ref, wz_ref, wg_ref,
                h_ref, qkv_ref, z_ref, gate_ref):
    h = x_ref[...] + jnp.dot(og_ref[...], wo_ref[...], preferred_element_type=F32)
    h_ref[...] = h
    hn = (h * _rms_scale(h) * lnw_ref[...]).astype(BF16)
    qkv_ref[...] = jnp.dot(hn, wqkv_ref[...], preferred_element_type=F32).astype(qkv_ref.dtype)
    z_ref[...] = jnp.dot(hn, wz_ref[...], preferred_element_type=F32).astype(z_ref.dtype)
    gate_ref[...] = jnp.dot(hn, wg_ref[...], preferred_element_type=F32)


def _mid_proj(x2, og2, wo, lnw, wqkv, wz, wg):
    n = x2.shape[0]
    tm = TOKEN_TILE // 2
    tok = lambda width: pl.BlockSpec((tm, width), lambda i: (i, 0))
    return pl.pallas_call(
        _mid_kernel,
        grid=(n // tm,),
        in_specs=[tok(D_MODEL), tok(NA_WIDTH), _const_spec(wo.shape), _const_spec((1, D_MODEL)),
                  _const_spec(wqkv.shape), _const_spec(wz.shape), _const_spec(wg.shape)],
        out_specs=[tok(D_MODEL), tok(GDN_CONV_CH), tok(GDN_VW), tok(GDN_GATE_LANES)],
        out_shape=[jax.ShapeDtypeStruct((n, D_MODEL), F32),
                   jax.ShapeDtypeStruct((n, GDN_CONV_CH), BF16),
                   jax.ShapeDtypeStruct((n, GDN_VW), BF16),
                   jax.ShapeDtypeStruct((n, GDN_GATE_LANES), F32)],
        compiler_params=_cparams(("parallel",)),
        name="mid_proj",
    )(x2, og2, wo, lnw, wqkv, wz, wg)


def _conv_kernel(prev_ref, cur_ref, next_ref, cw_ref, graw_ref, alog_ref, dtb_ref,
                 q_ref, k_ref, v_ref, gate_ref, ext_ref):
    i = pl.program_id(1)
    nblk = pl.num_programs(1)
    tb = GDN_BLOCK_TOKENS
    pad = GDN_CONV // 2
    halo = CONV_HALO
    grp = CONV_GROUP_ROWS
    ext_ref[0:halo, :] = jnp.where(i > 0, prev_ref[0], jnp.zeros_like(prev_ref[0]))
    ext_ref[halo:halo + tb, :] = cur_ref[0]
    ext_ref[halo + tb:, :] = jnp.where(i < nblk - 1, next_ref[0], jnp.zeros_like(next_ref[0]))

    taps = [j for j in range(GDN_CONV) if j != pad]
    srow = lax.broadcasted_iota(jnp.int32, (len(taps) * grp, grp + 2 * halo), 0)
    scol = lax.broadcasted_iota(jnp.int32, (len(taps) * grp, grp + 2 * halo), 1)
    tap_idx = srow // grp
    offset = jnp.where(tap_idx < pad, tap_idx - pad, tap_idx - pad + 1)
    shift = jnp.where(scol == halo + srow % grp + offset, 1.0, 0.0).astype(BF16)

    lane_chunk = CONV_LANE_CHUNK
    work = [(r, lc) for r in range(tb // grp) for lc in range(GDN_CONV_CH // lane_chunk)]

    def shifted(item):
        r, lc = item
        return _dot(shift, ext_ref[r * grp:r * grp + grp + 2 * halo, lc * lane_chunk:(lc + 1) * lane_chunk])

    pending = shifted(work[0])
    for n, (r, lc) in enumerate(work):
        y = pending
        if n + 1 < len(work):
            pending = shifted(work[n + 1])
        lanes = slice(lc * lane_chunk, (lc + 1) * lane_chunk)
        rows = slice(r * grp, (r + 1) * grp)
        acc = ext_ref[halo + r * grp:halo + (r + 1) * grp, lanes].astype(F32) * cw_ref[pad:pad + 1, lanes]
        for m, j in enumerate(taps):
            acc = acc + y[m * grp:(m + 1) * grp] * cw_ref[j:j + 1, lanes]
        act = _silu(acc)
        for sub in range(lane_chunk // GDN_DK):
            lo = lc * lane_chunk + sub * GDN_DK
            a = act[:, sub * GDN_DK:(sub + 1) * GDN_DK]
            if lo < 2 * GDN_KW:
                scale = GDN_DK ** -0.5 if lo < GDN_KW else 1.0
                a = a * (lax.rsqrt(jnp.sum(a * a, axis=-1, keepdims=True) + L2_EPS) * scale)
                o_ref = q_ref if lo < GDN_KW else k_ref
                col = lo % GDN_KW
            else:
                o_ref = v_ref
                col = lo - 2 * GDN_KW
            o_ref[0, rows, col:col + GDN_DK] = a.astype(o_ref.dtype)

    x = graw_ref[0]
    lane = lax.broadcasted_iota(jnp.int32, x.shape, 1)
    beta = jax.nn.sigmoid(x)
    decay = -jnp.exp(alog_ref[...]) * jax.nn.softplus(x + dtb_ref[...])
    for d in range(2):
        bsh = beta if d == 0 else pltpu.roll(beta, GDN_GATE_LANES - d * GDN_HEADS, 1)
        gsh = pltpu.roll(decay, GDN_GATE_LANES - (1 + d) * GDN_HEADS, 1)
        gate_ref[d, 0] = jnp.where(lane < GDN_HEADS, bsh, jnp.where(lane < 2 * GDN_HEADS, gsh, 0.0))


def _conv_gates(qkv, graw, conv_w, alog_l, dtb_l):
    b, t, _ = qkv.shape
    tb = GDN_BLOCK_TOKENS
    nblk = t // tb
    hb = tb // CONV_HALO
    nh = t // CONV_HALO
    cur = lambda width: pl.BlockSpec((1, tb, width), lambda bi, i: (bi, i, 0))
    prev = pl.BlockSpec((1, CONV_HALO, GDN_CONV_CH), lambda bi, i: (bi, jnp.maximum(i * hb - 1, 0), 0))
    nxt = pl.BlockSpec((1, CONV_HALO, GDN_CONV_CH),
                       lambda bi, i: (bi, jnp.minimum((i + 1) * hb, nh - 1), 0))
    return pl.pallas_call(
        _conv_kernel,
        grid=(b, nblk),
        in_specs=[prev, cur(GDN_CONV_CH), nxt, _const_spec(conv_w.shape), cur(GDN_GATE_LANES),
                  _const_spec((1, GDN_GATE_LANES)), _const_spec((1, GDN_GATE_LANES))],
        out_specs=[cur(GDN_KW), cur(GDN_KW), cur(GDN_VW),
                   pl.BlockSpec((2, 1, tb, GDN_GATE_LANES), lambda bi, i: (0, bi, i, 0))],
        out_shape=[jax.ShapeDtypeStruct((b, t, GDN_KW), BF16),
                   jax.ShapeDtypeStruct((b, t, GDN_KW), BF16),
                   jax.ShapeDtypeStruct((b, t, GDN_VW), BF16),
                   jax.ShapeDtypeStruct((2, b, t, GDN_GATE_LANES), F32)],
        scratch_shapes=[pltpu.VMEM((tb + 2 * CONV_HALO, GDN_CONV_CH), BF16)],
        compiler_params=_cparams(("parallel", "arbitrary")),
        name="conv_gates",
    )(qkv, qkv, qkv, conv_w, graw, alog_l, dtb_l)


def _split3(x):
    hi = x.astype(BF16)
    r1 = x - hi.astype(F32)
    mid = r1.astype(BF16)
    lo = (r1 - mid.astype(F32)).astype(BF16)
    return hi, mid, lo


def _dot(a, b):
    return jnp.dot(a, b, preferred_element_type=F32)


def _unit_triangular_inverses(a_list):
    c = a_list[0].shape[0]
    eye = (lax.broadcasted_iota(jnp.int32, (c, c), 0) == lax.broadcasted_iota(jnp.int32, (c, c), 1))
    xs = [jnp.where(eye, 1.0, 0.0) - a for a in a_list]
    abs_ = [a.astype(BF16) for a in a_list]
    ps = [_dot(ab, ab) for ab in abs_]
    for _ in range(int(np.log2(c)) - 2):
        pbs = [p.astype(BF16) for p in ps]
        xps = [_dot(jnp.concatenate([x.astype(BF16), pb], axis=0), pb) for x, pb in zip(xs, pbs)]
        xs = [x + xp[:c] for x, xp in zip(xs, xps)]
        ps = [xp[c:] for xp in xps]
    return [x + _dot(x.astype(BF16), p.astype(BF16)) for x, p in zip(xs, ps)]


def _delta_prepare(cc, ins, wq_ref, u_ref, aqk_ref, kd_ref, glast_ref):
    c = GDN_CHUNK
    rows = pl.ds(cc * c, c)
    row = lax.broadcasted_iota(jnp.int32, (c, c), 0)
    col = lax.broadcasted_iota(jnp.int32, (c, c), 1)
    nt = (((1,), (1,)), ((), ()))
    chains = []
    for direction, (q_ref, k_ref, v_ref, gate_ref) in enumerate(ins):
        causal = (row >= col) if direction == 0 else (row <= col)
        strict = (row > col) if direction == 0 else (row < col)
        tri = jnp.where(causal, 1.0, 0.0).astype(BF16)
        gates = gate_ref[0, 0, rows, :]
        hi, mid, lo = _split3(gates)
        gcum = _dot(tri, hi) + _dot(tri, mid) + _dot(tri, lo)
        gcum_t = gcum.T
        gtot = jnp.sum(gates, axis=0, keepdims=True)
        glast_ref[cc, direction] = jnp.broadcast_to(jnp.exp(gtot), (8, GDN_GATE_LANES))
        for h in range(GDN_HEADS):
            gl = GDN_HEADS + h
            chains.append(dict(
                causal=causal, strict=strict,
                beta=gates[:, h:h + 1], gc=gcum[:, gl:gl + 1], gr=gcum_t[gl:gl + 1, :],
                g_last=gtot[:, gl:gl + 1],
                k=k_ref[0, rows, h * GDN_DK:(h + 1) * GDN_DK],
                q=q_ref[0, rows, h * GDN_DK:(h + 1) * GDN_DK],
                v=v_ref[0, rows, h * GDN_DV:(h + 1) * GDN_DV]))
    for ch in chains:
        ch["kf"] = ch["k"].astype(F32)
        ch["kbeta"] = ch["kf"] * ch["beta"]
    kqs = [lax.dot_general(jnp.concatenate([ch["kbeta"].astype(BF16), ch["q"]], axis=0), ch["k"], nt,
                           preferred_element_type=F32) for ch in chains]
    a_list = []
    for ch, kq in zip(chains, kqs):
        decay = jnp.where(ch["causal"], jnp.exp(jnp.where(ch["causal"], ch["gc"] - ch["gr"], 0.0)), 0.0)
        a_list.append(jnp.where(ch["strict"], kq[:c] * decay, 0.0))
        ch["aqk"] = (kq[c:] * decay).astype(BF16)
    tinvs = _unit_triangular_inverses(a_list)
    rhss = []
    for ch in chains:
        ch["eg"] = jnp.exp(ch["gc"])
        rhss.append(jnp.concatenate([(ch["kbeta"] * ch["eg"]).astype(BF16),
                                     (ch["v"].astype(F32) * ch["beta"]).astype(BF16)], axis=1))
    wus = [_dot(t.astype(BF16), rhs) for t, rhs in zip(tinvs, rhss)]
    for g, (ch, wu) in enumerate(zip(chains, wus)):
        wq_ref[cc, g, 0:c, :] = wu[:, :GDN_DK].astype(BF16)
        wq_ref[cc, g, c:2 * c, :] = (ch["q"].astype(F32) * ch["eg"]).astype(BF16)
        u_ref[cc, g] = wu[:, GDN_DK:]
        aqk_ref[cc, g] = ch["aqk"]
        kd_ref[cc, g] = (ch["kf"] * jnp.exp(ch["g_last"] - ch["gc"])).astype(BF16)


def _delta_kernel(qf_ref, kf_ref, vf_ref, gf_ref, qb_ref, kb_ref, vb_ref, gb_ref,
                  of_ref, ob_ref, s_ref, wq_ref, u_ref, aqk_ref, kd_ref, glast_ref):
    i = pl.program_id(1)
    c = GDN_CHUNK
    n_chunks = GDN_BLOCK_TOKENS // c
    ins = ((qf_ref, kf_ref, vf_ref, gf_ref), (qb_ref, kb_ref, vb_ref, gb_ref))
    outs = (of_ref, ob_ref)

    @pl.when(i == 0)
    def _():
        s_ref[...] = jnp.zeros_like(s_ref)

    for cc in range(n_chunks):
        _delta_prepare(cc, ins, wq_ref, u_ref, aqk_ref, kd_ref, glast_ref)

    tn = (((0,), (0,)), ((), ()))
    n_chains = 2 * GDN_HEADS
    for ci in range(n_chunks):
        chunk_of = [ci if g < GDN_HEADS else n_chunks - 1 - ci for g in range(n_chains)]
        wss = [_dot(wq_ref[chunk_of[g], g], s_ref[g].astype(BF16)) for g in range(n_chains)]
        vnbs = [(u_ref[chunk_of[g], g] - wss[g][:c]).astype(BF16) for g in range(n_chains)]
        for g in range(n_chains):
            o = wss[g][c:] + _dot(aqk_ref[chunk_of[g], g], vnbs[g])
            h = g % GDN_HEADS
            outs[g // GDN_HEADS][0, pl.ds(chunk_of[g] * c, c), h * GDN_DV:(h + 1) * GDN_DV] = o.astype(BF16)
        for g in range(n_chains):
            gl = GDN_HEADS + g % GDN_HEADS
            e_last = glast_ref[chunk_of[g], g // GDN_HEADS, 0:1, gl:gl + 1]
            s_ref[g] = s_ref[g] * e_last + lax.dot_general(kd_ref[chunk_of[g], g], vnbs[g], tn,
                                                           preferred_element_type=F32)


def _gated_delta(q, k, v, gates):
    b, t, _ = q.shape
    tb = GDN_BLOCK_TOKENS
    nblk = t // tb
    n_chunks = tb // GDN_CHUNK
    n_chains = 2 * GDN_HEADS
    fwd = lambda width: pl.BlockSpec((1, tb, width), lambda bi, i: (bi, i, 0))
    bwd = lambda width: pl.BlockSpec((1, tb, width), lambda bi, i: (bi, nblk - 1 - i, 0))
    gate_f = pl.BlockSpec((1, 1, tb, GDN_GATE_LANES), lambda bi, i: (0, bi, i, 0))
    gate_b = pl.BlockSpec((1, 1, tb, GDN_GATE_LANES), lambda bi, i: (1, bi, nblk - 1 - i, 0))
    out = jax.ShapeDtypeStruct((b, t, GDN_VW), BF16)
    return pl.pallas_call(
        _delta_kernel,
        grid=(b, nblk),
        in_specs=[fwd(GDN_KW), fwd(GDN_KW), fwd(GDN_VW), gate_f,
                  bwd(GDN_KW), bwd(GDN_KW), bwd(GDN_VW), gate_b],
        out_specs=[fwd(GDN_VW), bwd(GDN_VW)],
        out_shape=[out, out],
        scratch_shapes=[pltpu.VMEM((n_chains, GDN_DK, GDN_DV), F32),
                        pltpu.VMEM((n_chunks, n_chains, 2 * GDN_CHUNK, GDN_DK), BF16),
                        pltpu.VMEM((n_chunks, n_chains, GDN_CHUNK, GDN_DV), F32),
                        pltpu.VMEM((n_chunks, n_chains, GDN_CHUNK, GDN_CHUNK), BF16),
                        pltpu.VMEM((n_chunks, n_chains, GDN_CHUNK, GDN_DK), BF16),
                        pltpu.VMEM((n_chunks, 2, 8, GDN_GATE_LANES), F32)],
        compiler_params=_cparams(("parallel", "arbitrary")),
        name="gated_delta",
    )(q, k, v, gates, q, k, v, gates)


def _out_kernel(of_ref, ob_ref, z_ref, h_ref, nw_ref, wo_ref, fw_ref, y_ref, og_ref):
    for hd in range(GDN_HEADS):
        lanes = slice(hd * GDN_DV, (hd + 1) * GDN_DV)
        o = of_ref[:, lanes].astype(F32) + ob_ref[:, lanes].astype(F32)
        z = z_ref[:, lanes].astype(F32)
        on = o * _rms_scale(o) * nw_ref[...]
        og_ref[:, lanes] = (on * _silu(z)).astype(og_ref.dtype)
    h = h_ref[...] + jnp.dot(og_ref[...], wo_ref[...], preferred_element_type=F32)
    y_ref[...] = h * _rms_scale(h) * fw_ref[...]


def _out_proj(of2, ob2, z2, h2, nw, wo, fw):
    n = h2.shape[0]
    tm = TOKEN_TILE
    tok = lambda width: pl.BlockSpec((tm, width), lambda i: (i, 0))
    return pl.pallas_call(
        _out_kernel,
        grid=(n // tm,),
        in_specs=[tok(GDN_VW), tok(GDN_VW), tok(GDN_VW), tok(D_MODEL),
                  _const_spec((1, GDN_DV)), _const_spec(wo.shape), _const_spec((1, D_MODEL))],
        out_specs=tok(D_MODEL),
        out_shape=jax.ShapeDtypeStruct((n, D_MODEL), F32),
        scratch_shapes=[pltpu.VMEM((tm, GDN_VW), BF16)],
        compiler_params=_cparams(("parallel",)),
        name="out_proj",
    )(of2, ob2, z2, h2, nw, wo, fw)


def _trunk(x, p):
    b, t, _ = x.shape
    n = b * t
    x2 = x.reshape(n, D_MODEL)
    q, k, v, g = _na_in_proj(x2, p["ln0"], p["na_w_in"])
    seq = lambda a: a.reshape(b, t, a.shape[-1])
    og = _neighbourhood_attention(seq(q), seq(k), seq(v), seq(g), p["na_bias"])
    h1, qkv, z, graw = _mid_proj(x2, og.reshape(n, NA_WIDTH), p["na_w_out"], p["ln1"],
                                 p["w_qkv"], p["w_z"], p["w_gate"])
    qn, kn, vn, gates = _conv_gates(seq(qkv), seq(graw), p["conv_w"], p["alog_l"], p["dtb_l"])
    o_f, o_b = _gated_delta(qn, kn, vn, gates)
    y = _out_proj(o_f.reshape(n, GDN_VW), o_b.reshape(n, GDN_VW), z, h1,
                  p["gdn_norm_w"], p["gdn_w_out"], p["final_w"])
    return y.reshape(b, t, D_MODEL)


def kernel(x_prompt, x_sample, ln_w, na_w_in, na_rpb, na_w_out, gdn_w_in, gdn_conv_w, gdn_a_log,
           gdn_dt_bias, gdn_norm_w, gdn_w_out, final_norm_w):
    o1 = GDN_CONV_CH
    o2 = o1 + GDN_VW
    w_in = gdn_w_in[0]
    n_gate = 4 * GDN_HEADS
    lane_pad = lambda a: jnp.pad(a, ((0, 0), (0, GDN_GATE_LANES - a.shape[1])))
    shift = lambda a: jnp.pad(a.reshape(1, 2 * GDN_HEADS).astype(F32),
                              ((0, 0), (2 * GDN_HEADS, GDN_GATE_LANES - n_gate)))
    p = {
        "ln0": ln_w[0].reshape(1, D_MODEL),
        "ln1": ln_w[1].reshape(1, D_MODEL),
        "na_w_in": na_w_in[0].astype(BF16),
        "na_w_out": na_w_out[0].astype(BF16),
        "w_qkv": w_in[:, :o1].astype(BF16),
        "w_z": w_in[:, o1:o2].astype(BF16),
        "w_gate": lane_pad(w_in[:, o2:]).astype(BF16),
        "conv_w": jnp.pad(gdn_conv_w[0], ((0, 8 - GDN_CONV), (0, 0))),
        "alog_l": shift(gdn_a_log[0]),
        "dtb_l": shift(gdn_dt_bias[0]),
        "gdn_norm_w": gdn_norm_w[0].reshape(1, GDN_DV),
        "gdn_w_out": gdn_w_out[0].astype(BF16),
        "final_w": final_norm_w.reshape(1, D_MODEL),
    }
    p["na_bias"] = _na_bias_tables(na_rpb[0])
    return tuple(_trunk(x, p) for x in (x_prompt, x_sample))
```

```python
import functools

import numpy as np
import jax
import jax.numpy as jnp
from jax import lax
from jax.experimental import pallas as pl
from jax.experimental.pallas import tpu as pltpu

F32 = jnp.float32
BF16 = jnp.bfloat16

D_MODEL = 1024
GRID_W = 64
RMS_EPS = 1e-6
L2_EPS = 1e-6
NEG_INF = -1e30

NA_HEADS = 16
NA_HEAD_DIM = 64
NA_WIDTH = NA_HEADS * NA_HEAD_DIM
NA_WIN_H = 8
NA_WIN_W = 16
NA_ROWS_PER_BLOCK = 4
NA_WIN_TOKENS = NA_WIN_H * GRID_W

GDN_HEADS = 8
GDN_DK = 128
GDN_DV = 256
GDN_KW = GDN_HEADS * GDN_DK
GDN_VW = GDN_HEADS * GDN_DV
GDN_CONV = 5
GDN_CONV_CH = 2 * GDN_KW + GDN_VW
GDN_CHUNK = 64
GDN_PREPARE_CHUNKS = 2
GDN_GATE_LANES = 128
GDN_BLOCK_TOKENS = 256
CONV_HALO = 16
CONV_GROUP_ROWS = 64
CONV_LANE_CHUNK = 512

TOKEN_TILE = 512
VMEM_LIMIT = 56 * 1024 * 1024


def _cparams(sem):
    return pltpu.CompilerParams(dimension_semantics=sem, vmem_limit_bytes=VMEM_LIMIT)


def _rms_scale(x):
    return lax.rsqrt(jnp.mean(x * x, axis=-1, keepdims=True) + RMS_EPS)


def _silu(x):
    return x * jax.nn.sigmoid(x)


def _const_spec(shape):
    nd = len(shape)
    return pl.BlockSpec(shape, lambda *_: (0,) * nd, pipeline_mode=pl.Buffered(1))


def _na_in_kernel(x_ref, lnw_ref, w_ref, q_ref, k_ref, v_ref, g_ref):
    x = x_ref[...]
    xn = (x * _rms_scale(x) * lnw_ref[...]).astype(BF16)
    outs = (q_ref, k_ref, v_ref, g_ref)
    for j, o_ref in enumerate(outs):
        y = jnp.dot(xn, w_ref[:, j * NA_WIDTH:(j + 1) * NA_WIDTH], preferred_element_type=F32)
        if j == 0:
            y = y * (NA_HEAD_DIM ** -0.5)
        o_ref[...] = y.astype(o_ref.dtype)


def _na_in_proj(x2, lnw, w):
    n = x2.shape[0]
    tm = TOKEN_TILE
    tok = lambda width: pl.BlockSpec((tm, width), lambda i: (i, 0))
    out = jax.ShapeDtypeStruct((n, NA_WIDTH), BF16)
    return pl.pallas_call(
        _na_in_kernel,
        grid=(n // tm,),
        in_specs=[tok(D_MODEL), _const_spec((1, D_MODEL)), _const_spec((D_MODEL, 4 * NA_WIDTH))],
        out_specs=[tok(NA_WIDTH)] * 4,
        out_shape=[out] * 4,
        compiler_params=_cparams(("parallel",)),
        name="na_in_proj",
    )(x2, lnw, w)


def _na_bias_tables(rpb):
    qc = np.arange(GRID_W)[:, None]
    kc = np.arange(GRID_W)[None, :]
    win_start = np.clip(qc - NA_WIN_W // 2, 0, GRID_W - NA_WIN_W)
    col_ok = (kc >= win_start) & (kc < win_start + NA_WIN_W)
    side = GRID_W - NA_WIN_W
    padded = jnp.pad(rpb.astype(F32), ((0, 0), (0, 0), (side, side)))
    toeplitz = jnp.stack([padded[:, :, GRID_W - 1 - c:2 * GRID_W - 1 - c] for c in range(GRID_W)],
                         axis=2)
    toeplitz = jnp.where(col_ok[None, None], toeplitz, NEG_INF)
    rb = NA_ROWS_PER_BLOCK
    d_all = list(range(rb)) + [NA_WIN_H // 2] * rb + [NA_WIN_H - rb + j for j in range(rb)]
    tables = []
    for d in d_all:
        t = toeplitz[:, NA_WIN_H - 1 - d:2 * NA_WIN_H - 1 - d]
        tables.append(t.transpose(0, 2, 1, 3).reshape(NA_HEADS, GRID_W, NA_WIN_TOKENS))
    return jnp.stack(tables)


def _na_kernel(q_ref, g_ref, kp_ref, kc_ref, kn_ref, vp_ref, vc_ref, vn_ref, bias_ref,
               o_ref, kcat_ref, vcat_ref, *, rows):
    rb = NA_ROWS_PER_BLOCK
    blk_tokens = rb * GRID_W
    i = pl.program_id(0)
    kcat_ref[0:blk_tokens, :] = kp_ref[0]
    kcat_ref[blk_tokens:2 * blk_tokens, :] = kc_ref[0]
    kcat_ref[2 * blk_tokens:3 * blk_tokens, :] = kn_ref[0]
    vcat_ref[0:blk_tokens, :] = vp_ref[0]
    vcat_ref[blk_tokens:2 * blk_tokens, :] = vc_ref[0]
    vcat_ref[2 * blk_tokens:3 * blk_tokens, :] = vn_ref[0]

    lane = lax.broadcasted_iota(jnp.int32, (GRID_W, 2 * NA_HEAD_DIM), 1)
    first_half = lane < NA_HEAD_DIM
    r0 = i * rb
    for qi in range(rb):
        r = r0 + qi
        rs = jnp.clip(r - NA_WIN_H // 2, 0, rows - NA_WIN_H)
        off = pl.multiple_of((rs - (r0 - rb)) * GRID_W, GRID_W)
        qrows = pl.ds(qi * GRID_W, GRID_W)
        krows = pl.ds(off, NA_WIN_TOKENS)
        nt = (((1,), (1,)), ((), ()))
        pair_lanes = [pl.ds(p * 2 * NA_HEAD_DIM, 2 * NA_HEAD_DIM) for p in range(NA_HEADS // 2)]
        ss = []
        for p, lanes in enumerate(pair_lanes):
            qp = q_ref[0, qrows, lanes]
            kw = kcat_ref[krows, lanes]
            zero = jnp.zeros_like(qp)
            q2 = jnp.concatenate([jnp.where(first_half, qp, zero), jnp.where(first_half, zero, qp)], axis=0)
            s2 = lax.dot_general(q2, kw, nt, preferred_element_type=F32)
            for half in range(2):
                ss.append(s2[half * GRID_W:(half + 1) * GRID_W] + bias_ref[qi, 2 * p + half])
        es, ls = [], []
        for s in ss:
            e = jnp.exp(s - jnp.max(s, axis=-1, keepdims=True))
            ls.append(jnp.sum(e, axis=-1, keepdims=True))
            es.append(e.astype(BF16))
        for p, lanes in enumerate(pair_lanes):
            vw = vcat_ref[krows, lanes]
            o2 = jnp.dot(jnp.concatenate(es[2 * p:2 * p + 2], axis=0), vw, preferred_element_type=F32)
            o_half = [o2[half * GRID_W:(half + 1) * GRID_W] / ls[2 * p + half] for half in range(2)]
            o = jnp.where(first_half, o_half[0], o_half[1])
            g = g_ref[0, qrows, lanes].astype(F32)
            o_ref[0, qrows, lanes] = (o * _silu(g)).astype(o_ref.dtype)


def _neighbourhood_attention(q, k, v, g, bias):
    b, t, _ = q.shape
    rows = t // GRID_W
    rb = NA_ROWS_PER_BLOCK
    assert rows % rb == 0 and rows >= 2 * NA_WIN_H
    nb = rows // rb
    bt = rb * GRID_W
    cur = pl.BlockSpec((1, bt, NA_WIDTH), lambda i, bi: (bi, i, 0))
    prev = pl.BlockSpec((1, bt, NA_WIDTH), lambda i, bi: (bi, jnp.maximum(i - 1, 0), 0))
    nxt = pl.BlockSpec((1, bt, NA_WIDTH), lambda i, bi: (bi, jnp.minimum(i + 1, nb - 1), 0))
    case = lambda i: jnp.where(i == 0, 0, jnp.where(i == nb - 1, 2, 1))
    bias_spec = pl.BlockSpec((rb, NA_HEADS, GRID_W, NA_WIN_TOKENS), lambda i, bi: (case(i), 0, 0, 0))
    return pl.pallas_call(
        functools.partial(_na_kernel, rows=rows),
        grid=(nb, b),
        in_specs=[cur, cur, prev, cur, nxt, prev, cur, nxt, bias_spec],
        out_specs=cur,
        out_shape=jax.ShapeDtypeStruct((b, t, NA_WIDTH), BF16),
        scratch_shapes=[pltpu.VMEM((3 * bt, NA_WIDTH), BF16), pltpu.VMEM((3 * bt, NA_WIDTH), BF16)],
        compiler_params=_cparams(("arbitrary", "arbitrary")),
        name="neighbourhood_attention",
    )(q, g, k, k, k, v, v, v, bias)


def _mid_kernel(x_ref, og_ref, wo_ref, lnw_ref, wqkv_ref, wz_ref, wg_ref,
                h_ref, qkv_ref, z_ref, gate_ref):
    h = x_ref[...] + jnp.dot(og_ref[...], wo_ref[...], preferred_element_type=F32)
    h_ref[...] = h
    hn = (h * _rms_scale(h) * lnw_ref[...]).astype(BF16)
    qkv_ref[...] = jnp.dot(hn, wqkv_ref[...], preferred_element_type=F32).astype(qkv_ref.dtype)
    z_ref[...] = jnp.dot(hn, wz_ref[...], preferred_element_type=F32).astype(z_ref.dtype)
    gate_ref[...] = jnp.dot(hn, wg_ref[...], preferred_element_type=F32)


def _mid_proj(x2, og2, wo, lnw, wqkv, wz, wg):
    n = x2.shape[0]
    tm = TOKEN_TILE // 2
    tok = lambda width: pl.BlockSpec((tm, width), lambda i: (i, 0))
    return pl.pallas_call(
        _mid_kernel,
        grid=(n // tm,),
        in_specs=[tok(D_MODEL), tok(NA_WIDTH), _const_spec(wo.shape), _const_spec((1, D_MODEL)),
                  _const_spec(wqkv.shape), _const_spec(wz.shape), _const_spec(wg.shape)],
        out_specs=[tok(D_MODEL), tok(GDN_CONV_CH), tok(GDN_VW), tok(GDN_GATE_LANES)],
        out_shape=[jax.ShapeDtypeStruct((n, D_MODEL), F32),
                   jax.ShapeDtypeStruct((n, GDN_CONV_CH), BF16),
                   jax.ShapeDtypeStruct((n, GDN_VW), BF16),
                   jax.ShapeDtypeStruct((n, GDN_GATE_LANES), F32)],
        compiler_params=_cparams(("parallel",)),
        name="mid_proj",
    )(x2, og2, wo, lnw, wqkv, wz, wg)


def _conv_kernel(prev_ref, cur_ref, next_ref, cw_ref, graw_ref, alog_ref, dtb_ref,
                 q_ref, k_ref, v_ref, gate_ref, ext_ref):
    i = pl.program_id(1)
    nblk = pl.num_programs(1)
    tb = GDN_BLOCK_TOKENS
    pad = GDN_CONV // 2
    halo = CONV_HALO
    grp = CONV_GROUP_ROWS
    ext_ref[0:halo, :] = jnp.where(i > 0, prev_ref[0], jnp.zeros_like(prev_ref[0]))
    ext_ref[halo:halo + tb, :] = cur_ref[0]
    ext_ref[halo + tb:, :] = jnp.where(i < nblk - 1, next_ref[0], jnp.zeros_like(next_ref[0]))

    taps = [j for j in range(GDN_CONV) if j != pad]
    srow = lax.broadcasted_iota(jnp.int32, (len(taps) * grp, grp + 2 * halo), 0)
    scol = lax.broadcasted_iota(jnp.int32, (len(taps) * grp, grp + 2 * halo), 1)
    tap_idx = srow // grp
    offset = jnp.where(tap_idx < pad, tap_idx - pad, tap_idx - pad + 1)
    shift = jnp.where(scol == halo + srow % grp + offset, 1.0, 0.0).astype(BF16)

    lane_chunk = CONV_LANE_CHUNK
    work = [(r, lc) for r in range(tb // grp) for lc in range(GDN_CONV_CH // lane_chunk)]

    def shifted(item):
        r, lc = item
        return _dot(shift, ext_ref[r * grp:r * grp + grp + 2 * halo, lc * lane_chunk:(lc + 1) * lane_chunk])

    pending = shifted(work[0])
    for n, (r, lc) in enumerate(work):
        y = pending
        if n + 1 < len(work):
            pending = shifted(work[n + 1])
        lanes = slice(lc * lane_chunk, (lc + 1) * lane_chunk)
        rows = slice(r * grp, (r + 1) * grp)
        acc = ext_ref[halo + r * grp:halo + (r + 1) * grp, lanes].astype(F32) * cw_ref[pad:pad + 1, lanes]
        for m, j in enumerate(taps):
            acc = acc + y[m * grp:(m + 1) * grp] * cw_ref[j:j + 1, lanes]
        act = _silu(acc)
        for sub in range(lane_chunk // GDN_DK):
            lo = lc * lane_chunk + sub * GDN_DK
            a = act[:, sub * GDN_DK:(sub + 1) * GDN_DK]
            if lo < 2 * GDN_KW:
                scale = GDN_DK ** -0.5 if lo < GDN_KW else 1.0
                a = a * (lax.rsqrt(jnp.sum(a * a, axis=-1, keepdims=True) + L2_EPS) * scale)
                o_ref = q_ref if lo < GDN_KW else k_ref
                col = lo % GDN_KW
            else:
                o_ref = v_ref
                col = lo - 2 * GDN_KW
            o_ref[0, rows, col:col + GDN_DK] = a.astype(o_ref.dtype)

    x = graw_ref[0]
    lane = lax.broadcasted_iota(jnp.int32, x.shape, 1)
    beta = jax.nn.sigmoid(x)
    decay = -jnp.exp(alog_ref[...]) * jax.nn.softplus(x + dtb_ref[...])
    for d in range(2):
        bsh = beta if d == 0 else pltpu.roll(beta, GDN_GATE_LANES - d * GDN_HEADS, 1)
        gsh = pltpu.roll(decay, GDN_GATE_LANES - (1 + d) * GDN_HEADS, 1)
        gate_ref[d, 0] = jnp.where(lane < GDN_HEADS, bsh, jnp.where(lane < 2 * GDN_HEADS, gsh, 0.0))


def _conv_gates(qkv, graw, conv_w, alog_l, dtb_l):
    b, t, _ = qkv.shape
    tb = GDN_BLOCK_TOKENS
    nblk = t // tb
    hb = tb // CONV_HALO
    nh = t // CONV_HALO
    cur = lambda width: pl.BlockSpec((1, tb, width), lambda bi, i: (bi, i, 0))
    prev = pl.BlockSpec((1, CONV_HALO, GDN_CONV_CH), lambda bi, i: (bi, jnp.maximum(i * hb - 1, 0), 0))
    nxt = pl.BlockSpec((1, CONV_HALO, GDN_CONV_CH),
                       lambda bi, i: (bi, jnp.minimum((i + 1) * hb, nh - 1), 0))
    return pl.pallas_call(
        _conv_kernel,
        grid=(b, nblk),
        in_specs=[prev, cur(GDN_CONV_CH), nxt, _const_spec(conv_w.shape), cur(GDN_GATE_LANES),
                  _const_spec((1, GDN_GATE_LANES)), _const_spec((1, GDN_GATE_LANES))],
        out_specs=[cur(GDN_KW), cur(GDN_KW), cur(GDN_VW),
                   pl.BlockSpec((2, 1, tb, GDN_GATE_LANES), lambda bi, i: (0, bi, i, 0))],
        out_shape=[jax.ShapeDtypeStruct((b, t, GDN_KW), BF16),
                   jax.ShapeDtypeStruct((b, t, GDN_KW), BF16),
                   jax.ShapeDtypeStruct((b, t, GDN_VW), BF16),
                   jax.ShapeDtypeStruct((2, b, t, GDN_GATE_LANES), F32)],
        scratch_shapes=[pltpu.VMEM((tb + 2 * CONV_HALO, GDN_CONV_CH), BF16)],
        compiler_params=_cparams(("parallel", "arbitrary")),
        name="conv_gates",
    )(qkv, qkv, qkv, conv_w, graw, alog_l, dtb_l)


def _split3(x):
    hi = x.astype(BF16)
    r1 = x - hi.astype(F32)
    mid = r1.astype(BF16)
    lo = (r1 - mid.astype(F32)).astype(BF16)
    return hi, mid, lo


def _dot(a, b):
    return jnp.dot(a, b, preferred_element_type=F32)


def _unit_triangular_inverses(a_list):
    c = a_list[0].shape[0]
    row = lax.broadcasted_iota(jnp.int32, (c, 2 * c), 0)
    col = lax.broadcasted_iota(jnp.int32, (c, 2 * c), 1)
    x_half = col >= c
    eye = jnp.where(col == row + c, 1.0, 0.0)
    zs = [jnp.where(x_half, eye, -jnp.concatenate([a, a], axis=1)) for a in a_list]
    for _ in range(int(np.log2(c))):
        zbs = [z.astype(BF16) for z in zs]
        zs = [_dot(zb[:, :c], zb) + jnp.where(x_half, z, 0.0) for z, zb in zip(zs, zbs)]
    return [z[:, c:] for z in zs]


def _delta_prepare(chunks, ins, wq_ref, u_ref, aqk_ref, kd_ref, glast_ref):
    c = GDN_CHUNK
    row = lax.broadcasted_iota(jnp.int32, (c, c), 0)
    col = lax.broadcasted_iota(jnp.int32, (c, c), 1)
    nt = (((1,), (1,)), ((), ()))
    chains = []
    for cc in chunks:
        rows = pl.ds(cc * c, c)
        for direction, (q_ref, k_ref, v_ref, gate_ref) in enumerate(ins):
            causal = (row >= col) if direction == 0 else (row <= col)
            strict = (row > col) if direction == 0 else (row < col)
            tri = jnp.where(causal, 1.0, 0.0).astype(BF16)
            gates = gate_ref[0, 0, rows, :]
            hi, mid, lo = _split3(gates)
            gcum = _dot(tri, hi) + _dot(tri, mid) + _dot(tri, lo)
            gcum_t = gcum.T
            gtot = jnp.sum(gates, axis=0, keepdims=True)
            glast_ref[cc, direction] = jnp.broadcast_to(jnp.exp(gtot), (8, GDN_GATE_LANES))
            for h in range(GDN_HEADS):
                gl = GDN_HEADS + h
                chains.append(dict(
                    slot=(cc, direction * GDN_HEADS + h), causal=causal, strict=strict,
                    beta=gates[:, h:h + 1], gc=gcum[:, gl:gl + 1], gr=gcum_t[gl:gl + 1, :],
                    g_last=gtot[:, gl:gl + 1],
                    k=k_ref[0, rows, h * GDN_DK:(h + 1) * GDN_DK],
                    q=q_ref[0, rows, h * GDN_DK:(h + 1) * GDN_DK],
                    v=v_ref[0, rows, h * GDN_DV:(h + 1) * GDN_DV]))
    for ch in chains:
        ch["kf"] = ch["k"].astype(F32)
        ch["kbeta"] = ch["kf"] * ch["beta"]
    kqs = [lax.dot_general(jnp.concatenate([ch["kbeta"].astype(BF16), ch["q"]], axis=0), ch["k"], nt,
                           preferred_element_type=F32) for ch in chains]
    a_list = []
    for ch, kq in zip(chains, kqs):
        decay = jnp.where(ch["causal"], jnp.exp(jnp.where(ch["causal"], ch["gc"] - ch["gr"], 0.0)), 0.0)
        a_list.append(jnp.where(ch["strict"], kq[:c] * decay, 0.0))
        ch["aqk"] = (kq[c:] * decay).astype(BF16)
    tinvs = _unit_triangular_inverses(a_list)
    rhss = []
    for ch in chains:
        ch["eg"] = jnp.exp(ch["gc"])
        rhss.append(jnp.concatenate([(ch["kbeta"] * ch["eg"]).astype(BF16),
                                     (ch["v"].astype(F32) * ch["beta"]).astype(BF16)], axis=1))
    wus = [_dot(t.astype(BF16), rhs) for t, rhs in zip(tinvs, rhss)]
    for ch, wu in zip(chains, wus):
        cc, g = ch["slot"]
        wq_ref[cc, g, 0:c, :] = wu[:, :GDN_DK].astype(BF16)
        wq_ref[cc, g, c:2 * c, :] = (ch["q"].astype(F32) * ch["eg"]).astype(BF16)
        u_ref[cc, g] = wu[:, GDN_DK:]
        aqk_ref[cc, g] = ch["aqk"]
        kd_ref[cc, g] = (ch["kf"] * jnp.exp(ch["g_last"] - ch["gc"])).astype(BF16)


def _delta_kernel(qf_ref, kf_ref, vf_ref, gf_ref, qb_ref, kb_ref, vb_ref, gb_ref,
                  of_ref, ob_ref, s_ref, wq_ref, u_ref, aqk_ref, kd_ref, glast_ref):
    i = pl.program_id(1)
    c = GDN_CHUNK
    n_chunks = GDN_BLOCK_TOKENS // c
    ins = ((qf_ref, kf_ref, vf_ref, gf_ref), (qb_ref, kb_ref, vb_ref, gb_ref))
    outs = (of_ref, ob_ref)

    @pl.when(i == 0)
    def _():
        s_ref[...] = jnp.zeros_like(s_ref)

    for c0 in range(0, n_chunks, GDN_PREPARE_CHUNKS):
        _delta_prepare(range(c0, c0 + GDN_PREPARE_CHUNKS), ins, wq_ref, u_ref, aqk_ref, kd_ref, glast_ref)

    tn = (((0,), (0,)), ((), ()))
    n_chains = 2 * GDN_HEADS
    for ci in range(n_chunks):
        chunk_of = [ci if g < GDN_HEADS else n_chunks - 1 - ci for g in range(n_chains)]
        wss = [_dot(wq_ref[chunk_of[g], g], s_ref[g].astype(BF16)) for g in range(n_chains)]
        vnbs = [(u_ref[chunk_of[g], g] - wss[g][:c]).astype(BF16) for g in range(n_chains)]
        for g in range(n_chains):
            o = wss[g][c:] + _dot(aqk_ref[chunk_of[g], g], vnbs[g])
            h = g % GDN_HEADS
            outs[g // GDN_HEADS][0, pl.ds(chunk_of[g] * c, c), h * GDN_DV:(h + 1) * GDN_DV] = o.astype(BF16)
        for g in range(n_chains):
            gl = GDN_HEADS + g % GDN_HEADS
            e_last = glast_ref[chunk_of[g], g // GDN_HEADS, 0:1, gl:gl + 1]
            s_ref[g] = s_ref[g] * e_last + lax.dot_general(kd_ref[chunk_of[g], g], vnbs[g], tn,
                                                           preferred_element_type=F32)


def _gated_delta(q, k, v, gates):
    b, t, _ = q.shape
    tb = GDN_BLOCK_TOKENS
    nblk = t // tb
    n_chunks = tb // GDN_CHUNK
    n_chains = 2 * GDN_HEADS
    fwd = lambda width: pl.BlockSpec((1, tb, width), lambda bi, i: (bi, i, 0))
    bwd = lambda width: pl.BlockSpec((1, tb, width), lambda bi, i: (bi, nblk - 1 - i, 0))
    gate_f = pl.BlockSpec((1, 1, tb, GDN_GATE_LANES), lambda bi, i: (0, bi, i, 0))
    gate_b = pl.BlockSpec((1, 1, tb, GDN_GATE_LANES), lambda bi, i: (1, bi, nblk - 1 - i, 0))
    out = jax.ShapeDtypeStruct((b, t, GDN_VW), BF16)
    return pl.pallas_call(
        _delta_kernel,
        grid=(b, nblk),
        in_specs=[fwd(GDN_KW), fwd(GDN_KW), fwd(GDN_VW), gate_f,
                  bwd(GDN_KW), bwd(GDN_KW), bwd(GDN_VW), gate_b],
        out_specs=[fwd(GDN_VW), bwd(GDN_VW)],
        out_shape=[out, out],
        scratch_shapes=[pltpu.VMEM((n_chains, GDN_DK, GDN_DV), F32),
                        pltpu.VMEM((n_chunks, n_chains, 2 * GDN_CHUNK, GDN_DK), BF16),
                        pltpu.VMEM((n_chunks, n_chains, GDN_CHUNK, GDN_DV), F32),
                        pltpu.VMEM((n_chunks, n_chains, GDN_CHUNK, GDN_CHUNK), BF16),
                        pltpu.VMEM((n_chunks, n_chains, GDN_CHUNK, GDN_DK), BF16),
                        pltpu.VMEM((n_chunks, 2, 8, GDN_GATE_LANES), F32)],
        compiler_params=_cparams(("parallel", "arbitrary")),
        name="gated_delta",
    )(q, k, v, gates, q, k, v, gates)


def _out_kernel(of_ref, ob_ref, z_ref, h_ref, nw_ref, wo_ref, fw_ref, y_ref, og_ref):
    for hd in range(GDN_HEADS):
        lanes = slice(hd * GDN_DV, (hd + 1) * GDN_DV)
        o = of_ref[:, lanes].astype(F32) + ob_ref[:, lanes].astype(F32)
        z = z_ref[:, lanes].astype(F32)
        on = o * _rms_scale(o) * nw_ref[...]
        og_ref[:, lanes] = (on * _silu(z)).astype(og_ref.dtype)
    h = h_ref[...] + jnp.dot(og_ref[...], wo_ref[...], preferred_element_type=F32)
    y_ref[...] = h * _rms_scale(h) * fw_ref[...]


def _out_proj(of2, ob2, z2, h2, nw, wo, fw):
    n = h2.shape[0]
    tm = TOKEN_TILE
    tok = lambda width: pl.BlockSpec((tm, width), lambda i: (i, 0))
    return pl.pallas_call(
        _out_kernel,
        grid=(n // tm,),
        in_specs=[tok(GDN_VW), tok(GDN_VW), tok(GDN_VW), tok(D_MODEL),
                  _const_spec((1, GDN_DV)), _const_spec(wo.shape), _const_spec((1, D_MODEL))],
        out_specs=tok(D_MODEL),
        out_shape=jax.ShapeDtypeStruct((n, D_MODEL), F32),
        scratch_shapes=[pltpu.VMEM((tm, GDN_VW), BF16)],
        compiler_params=_cparams(("parallel",)),
        name="out_proj",
    )(of2, ob2, z2, h2, nw, wo, fw)


def _trunk(x, p):
    b, t, _ = x.shape
    n = b * t
    x2 = x.reshape(n, D_MODEL)
    q, k, v, g = _na_in_proj(x2, p["ln0"], p["na_w_in"])
    seq = lambda a: a.reshape(b, t, a.shape[-1])
    og = _neighbourhood_attention(seq(q), seq(k), seq(v), seq(g), p["na_bias"])
    h1, qkv, z, graw = _mid_proj(x2, og.reshape(n, NA_WIDTH), p["na_w_out"], p["ln1"],
                                 p["w_qkv"], p["w_z"], p["w_gate"])
    qn, kn, vn, gates = _conv_gates(seq(qkv), seq(graw), p["conv_w"], p["alog_l"], p["dtb_l"])
    o_f, o_b = _gated_delta(qn, kn, vn, gates)
    y = _out_proj(o_f.reshape(n, GDN_VW), o_b.reshape(n, GDN_VW), z, h1,
                  p["gdn_norm_w"], p["gdn_w_out"], p["final_w"])
    return y.reshape(b, t, D_MODEL)


def kernel(x_prompt, x_sample, ln_w, na_w_in, na_rpb, na_w_out, gdn_w_in, gdn_conv_w, gdn_a_log,
           gdn_dt_bias, gdn_norm_w, gdn_w_out, final_norm_w):
    o1 = GDN_CONV_CH
    o2 = o1 + GDN_VW
    w_in = gdn_w_in[0]
    n_gate = 4 * GDN_HEADS
    lane_pad = lambda a: jnp.pad(a, ((0, 0), (0, GDN_GATE_LANES - a.shape[1])))
    shift = lambda a: jnp.pad(a.reshape(1, 2 * GDN_HEADS).astype(F32),
                              ((0, 0), (2 * GDN_HEADS, GDN_GATE_LANES - n_gate)))
    p = {
        "ln0": ln_w[0].reshape(1, D_MODEL),
        "ln1": ln_w[1].reshape(1, D_MODEL),
        "na_w_in": na_w_in[0].astype(BF16),
        "na_w_out": na_w_out[0].astype(BF16),
        "w_qkv": w_in[:, :o1].astype(BF16),
        "w_z": w_in[:, o1:o2].astype(BF16),
        "w_gate": lane_pad(w_in[:, o2:]).astype(BF16),
        "conv_w": jnp.pad(gdn_conv_w[0], ((0, 8 - GDN_CONV), (0, 0))),
        "alog_l": shift(gdn_a_log[0]),
        "dtb_l": shift(gdn_dt_bias[0]),
        "gdn_norm_w": gdn_norm_w[0].reshape(1, GDN_DV),
        "gdn_w_out": gdn_w_out[0].astype(BF16),
        "final_w": final_norm_w.reshape(1, D_MODEL),
    }
    p["na_bias"] = _na_bias_tables(na_rpb[0])
    return tuple(_trunk(x, p) for x in (x_prompt, x_sample))
```

```python
import functools

import numpy as np
import jax
import jax.numpy as jnp
from jax import lax
from jax.experimental import pallas as pl
from jax.experimental.pallas import tpu as pltpu

F32 = jnp.float32
BF16 = jnp.bfloat16

D_MODEL = 1024
GRID_W = 64
RMS_EPS = 1e-6
L2_EPS = 1e-6
NEG_INF = -1e30

NA_HEADS = 16
NA_HEAD_DIM = 64
NA_WIDTH = NA_HEADS * NA_HEAD_DIM
NA_WIN_H = 8
NA_WIN_W = 16
NA_ROWS_PER_BLOCK = 4
NA_WIN_TOKENS = NA_WIN_H * GRID_W

GDN_HEADS = 8
GDN_DK = 128
GDN_DV = 256
GDN_KW = GDN_HEADS * GDN_DK
GDN_VW = GDN_HEADS * GDN_DV
GDN_CONV = 5
GDN_CONV_CH = 2 * GDN_KW + GDN_VW
GDN_CHUNK = 64
GDN_PREPARE_CHUNKS = 2
GDN_SCAN_CHAINS = 16
GDN_GATE_LANES = 128
GDN_BLOCK_TOKENS = 512
CONV_BLOCK_TOKENS = 512
CONV_HALO = 16
CONV_GROUP_ROWS = 64
CONV_LANE_CHUNK = 512

TOKEN_TILE = 512
VMEM_LIMIT = 56 * 1024 * 1024


def _cparams(sem):
    return pltpu.CompilerParams(dimension_semantics=sem, vmem_limit_bytes=VMEM_LIMIT)


def _rms_scale(x):
    return lax.rsqrt(jnp.mean(x * x, axis=-1, keepdims=True) + RMS_EPS)


def _silu(x):
    return x * jax.nn.sigmoid(x)


def _const_spec(shape):
    nd = len(shape)
    return pl.BlockSpec(shape, lambda *_: (0,) * nd, pipeline_mode=pl.Buffered(1))


def _na_in_kernel(x_ref, lnw_ref, w_ref, qg_ref, kv_ref):
    x = x_ref[...]
    xn = (x * _rms_scale(x) * lnw_ref[...]).astype(BF16)
    dests = ((qg_ref, 0), (kv_ref, 0), (kv_ref, NA_WIDTH), (qg_ref, NA_WIDTH))
    for j, (o_ref, col) in enumerate(dests):
        y = jnp.dot(xn, w_ref[:, j * NA_WIDTH:(j + 1) * NA_WIDTH], preferred_element_type=F32)
        if j == 0:
            y = y * (NA_HEAD_DIM ** -0.5)
        o_ref[:, col:col + NA_WIDTH] = y.astype(o_ref.dtype)


def _na_in_proj(x2, lnw, w):
    n = x2.shape[0]
    tm = TOKEN_TILE
    tok = lambda width: pl.BlockSpec((tm, width), lambda i: (i, 0))
    out = jax.ShapeDtypeStruct((n, 2 * NA_WIDTH), BF16)
    return pl.pallas_call(
        _na_in_kernel,
        grid=(n // tm,),
        in_specs=[tok(D_MODEL), _const_spec((1, D_MODEL)), _const_spec((D_MODEL, 4 * NA_WIDTH))],
        out_specs=[tok(2 * NA_WIDTH)] * 2,
        out_shape=[out] * 2,
        compiler_params=_cparams(("parallel",)),
        name="na_in_proj",
    )(x2, lnw, w)


def _na_bias_tables(rpb):
    qc = np.arange(GRID_W)[:, None]
    kc = np.arange(GRID_W)[None, :]
    win_start = np.clip(qc - NA_WIN_W // 2, 0, GRID_W - NA_WIN_W)
    col_ok = (kc >= win_start) & (kc < win_start + NA_WIN_W)
    side = GRID_W - NA_WIN_W
    padded = jnp.pad(rpb.astype(F32), ((0, 0), (0, 0), (side, side)))
    toeplitz = jnp.stack([padded[:, :, GRID_W - 1 - c:2 * GRID_W - 1 - c] for c in range(GRID_W)],
                         axis=2)
    toeplitz = jnp.where(col_ok[None, None], toeplitz, NEG_INF)
    rb = NA_ROWS_PER_BLOCK
    d_all = list(range(rb)) + [NA_WIN_H // 2] * rb + [NA_WIN_H - rb + j for j in range(rb)]
    tables = []
    for d in d_all:
        t = toeplitz[:, NA_WIN_H - 1 - d:2 * NA_WIN_H - 1 - d]
        tables.append(t.transpose(0, 2, 1, 3).reshape(NA_HEADS, GRID_W, NA_WIN_TOKENS))
    return jnp.stack(tables)


def _na_kernel(qg_ref, kvp_ref, kvc_ref, kvn_ref, bias_ref, o_ref, kvcat_ref, *, rows):
    rb = NA_ROWS_PER_BLOCK
    blk_tokens = rb * GRID_W
    i = pl.program_id(0)
    kvcat_ref[0:blk_tokens, :] = kvp_ref[0]
    kvcat_ref[blk_tokens:2 * blk_tokens, :] = kvc_ref[0]
    kvcat_ref[2 * blk_tokens:3 * blk_tokens, :] = kvn_ref[0]

    lane = lax.broadcasted_iota(jnp.int32, (GRID_W, 2 * NA_HEAD_DIM), 1)
    first_half = lane < NA_HEAD_DIM
    r0 = i * rb
    for qi in range(rb):
        r = r0 + qi
        rs = jnp.clip(r - NA_WIN_H // 2, 0, rows - NA_WIN_H)
        off = pl.multiple_of((rs - (r0 - rb)) * GRID_W, GRID_W)
        qrows = pl.ds(qi * GRID_W, GRID_W)
        krows = pl.ds(off, NA_WIN_TOKENS)
        nt = (((1,), (1,)), ((), ()))
        pair_lanes = [pl.ds(p * 2 * NA_HEAD_DIM, 2 * NA_HEAD_DIM) for p in range(NA_HEADS // 2)]
        pair_lanes_hi = [pl.ds(NA_WIDTH + p * 2 * NA_HEAD_DIM, 2 * NA_HEAD_DIM) for p in range(NA_HEADS // 2)]
        ss = []
        for p, lanes in enumerate(pair_lanes):
            qp = qg_ref[0, qrows, lanes]
            kw = kvcat_ref[krows, lanes]
            zero = jnp.zeros_like(qp)
            q2 = jnp.concatenate([jnp.where(first_half, qp, zero), jnp.where(first_half, zero, qp)], axis=0)
            s2 = lax.dot_general(q2, kw, nt, preferred_element_type=F32)
            for half in range(2):
                ss.append(s2[half * GRID_W:(half + 1) * GRID_W] + bias_ref[qi, 2 * p + half])
        es, ls = [], []
        for s in ss:
            e = jnp.exp(s - jnp.max(s, axis=-1, keepdims=True))
            ls.append(jnp.sum(e, axis=-1, keepdims=True))
            es.append(e.astype(BF16))
        for p, lanes in enumerate(pair_lanes):
            vw = kvcat_ref[krows, pair_lanes_hi[p]]
            o2 = jnp.dot(jnp.concatenate(es[2 * p:2 * p + 2], axis=0), vw, preferred_element_type=F32)
            o_half = [o2[half * GRID_W:(half + 1) * GRID_W] / ls[2 * p + half] for half in range(2)]
            o = jnp.where(first_half, o_half[0], o_half[1])
            g = qg_ref[0, qrows, pair_lanes_hi[p]].astype(F32)
            o_ref[0, qrows, lanes] = (o * _silu(g)).astype(o_ref.dtype)


def _neighbourhood_attention(qg, kv, bias):
    b, t, _ = qg.shape
    rows = t // GRID_W
    rb = NA_ROWS_PER_BLOCK
    assert rows % rb == 0 and rows >= 2 * NA_WIN_H
    nb = rows // rb
    bt = rb * GRID_W
    cur = lambda width: pl.BlockSpec((1, bt, width), lambda i, bi: (bi, i, 0))
    prev = pl.BlockSpec((1, bt, 2 * NA_WIDTH), lambda i, bi: (bi, jnp.maximum(i - 1, 0), 0))
    nxt = pl.BlockSpec((1, bt, 2 * NA_WIDTH), lambda i, bi: (bi, jnp.minimum(i + 1, nb - 1), 0))
    case = lambda i: jnp.where(i == 0, 0, jnp.where(i == nb - 1, 2, 1))
    bias_spec = pl.BlockSpec((rb, NA_HEADS, GRID_W, NA_WIN_TOKENS), lambda i, bi: (case(i), 0, 0, 0))
    return pl.pallas_call(
        functools.partial(_na_kernel, rows=rows),
        grid=(nb, b),
        in_specs=[cur(2 * NA_WIDTH), prev, cur(2 * NA_WIDTH), nxt, bias_spec],
        out_specs=cur(NA_WIDTH),
        out_shape=jax.ShapeDtypeStruct((b, t, NA_WIDTH), BF16),
        scratch_shapes=[pltpu.VMEM((3 * bt, 2 * NA_WIDTH), BF16)],
        compiler_params=_cparams(("arbitrary", "arbitrary")),
        name="neighbourhood_attention",
    )(qg, kv, kv, kv, bias)


def _mid_kernel(x_ref, og_ref, wo_ref, lnw_ref, wqkv_ref, wz_ref, wg_ref,
                h_ref, qkv_ref, z_ref, gate_ref):
    h = x_ref[...] + jnp.dot(og_ref[...], wo_ref[...], preferred_element_type=F32)
    h_ref[...] = h
    hn = (h * _rms_scale(h) * lnw_ref[...]).astype(BF16)
    qkv_ref[...] = jnp.dot(hn, wqkv_ref[...], preferred_element_type=F32).astype(qkv_ref.dtype)
    z_ref[...] = jnp.dot(hn, wz_ref[...], preferred_element_type=F32).astype(z_ref.dtype)
    gate_ref[...] = jnp.dot(hn, wg_ref[...], preferred_element_type=F32)


def _mid_proj(x2, og2, wo, lnw, wqkv, wz, wg):
    n = x2.shape[0]
    tm = TOKEN_TILE // 2
    tok = lambda width: pl.BlockSpec((tm, width), lambda i: (i, 0))
    return pl.pallas_call(
        _mid_kernel,
        grid=(n // tm,),
        in_specs=[tok(D_MODEL), tok(NA_WIDTH), _const_spec(wo.shape), _const_spec((1, D_MODEL)),
                  _const_spec(wqkv.shape), _const_spec(wz.shape), _const_spec(wg.shape)],
        out_specs=[tok(D_MODEL), tok(GDN_CONV_CH), tok(GDN_VW), tok(GDN_GATE_LANES)],
        out_shape=[jax.ShapeDtypeStruct((n, D_MODEL), F32),
                   jax.ShapeDtypeStruct((n, GDN_CONV_CH), BF16),
                   jax.ShapeDtypeStruct((n, GDN_VW), BF16),
                   jax.ShapeDtypeStruct((n, GDN_GATE_LANES), F32)],
        compiler_params=_cparams(("parallel",)),
        name="mid_proj",
    )(x2, og2, wo, lnw, wqkv, wz, wg)


def _conv_kernel(prev_ref, cur_ref, next_ref, cw_ref, graw_ref, alog_ref, dtb_ref,
                 qkv_ref, gate_ref, ext_ref):
    i = pl.program_id(1)
    nblk = pl.num_programs(1)
    tb = CONV_BLOCK_TOKENS
    pad = GDN_CONV // 2
    halo = CONV_HALO
    grp = CONV_GROUP_ROWS
    ext_ref[0:halo, :] = jnp.where(i > 0, prev_ref[0], jnp.zeros_like(prev_ref[0]))
    ext_ref[halo:halo + tb, :] = cur_ref[0]
    ext_ref[halo + tb:, :] = jnp.where(i < nblk - 1, next_ref[0], jnp.zeros_like(next_ref[0]))

    taps = [j for j in range(GDN_CONV) if j != pad]
    srow = lax.broadcasted_iota(jnp.int32, (len(taps) * grp, grp + 2 * halo), 0)
    scol = lax.broadcasted_iota(jnp.int32, (len(taps) * grp, grp + 2 * halo), 1)
    tap_idx = srow // grp
    offset = jnp.where(tap_idx < pad, tap_idx - pad, tap_idx - pad + 1)
    shift = jnp.where(scol == halo + srow % grp + offset, 1.0, 0.0).astype(BF16)

    lane_chunk = CONV_LANE_CHUNK
    work = [(r, lc) for r in range(tb // grp) for lc in range(GDN_CONV_CH // lane_chunk)]

    def shifted(item):
        r, lc = item
        return _dot(shift, ext_ref[r * grp:r * grp + grp + 2 * halo, lc * lane_chunk:(lc + 1) * lane_chunk])

    pending = shifted(work[0])
    for n, (r, lc) in enumerate(work):
        y = pending
        if n + 1 < len(work):
            pending = shifted(work[n + 1])
        lanes = slice(lc * lane_chunk, (lc + 1) * lane_chunk)
        rows = slice(r * grp, (r + 1) * grp)
        acc = ext_ref[halo + r * grp:halo + (r + 1) * grp, lanes].astype(F32) * cw_ref[pad:pad + 1, lanes]
        for m, j in enumerate(taps):
            acc = acc + y[m * grp:(m + 1) * grp] * cw_ref[j:j + 1, lanes]
        act = _silu(acc)
        for sub in range(lane_chunk // GDN_DK):
            lo = lc * lane_chunk + sub * GDN_DK
            a = act[:, sub * GDN_DK:(sub + 1) * GDN_DK]
            if lo < 2 * GDN_KW:
                scale = GDN_DK ** -0.5 if lo < GDN_KW else 1.0
                a = a * (lax.rsqrt(jnp.sum(a * a, axis=-1, keepdims=True) + L2_EPS) * scale)
            qkv_ref[0, rows, lo:lo + GDN_DK] = a.astype(qkv_ref.dtype)

    x = graw_ref[0]
    lane = lax.broadcasted_iota(jnp.int32, x.shape, 1)
    beta = jax.nn.sigmoid(x)
    decay = -jnp.exp(alog_ref[...]) * jax.nn.softplus(x + dtb_ref[...])
    for d in range(2):
        bsh = beta if d == 0 else pltpu.roll(beta, GDN_GATE_LANES - d * GDN_HEADS, 1)
        gsh = pltpu.roll(decay, GDN_GATE_LANES - (1 + d) * GDN_HEADS, 1)
        gate_ref[d, 0] = jnp.where(lane < GDN_HEADS, bsh, jnp.where(lane < 2 * GDN_HEADS, gsh, 0.0))


def _conv_gates(qkv, graw, conv_w, alog_l, dtb_l):
    b, t, _ = qkv.shape
    tb = CONV_BLOCK_TOKENS
    nblk = t // tb
    hb = tb // CONV_HALO
    nh = t // CONV_HALO
    cur = lambda width: pl.BlockSpec((1, tb, width), lambda bi, i: (bi, i, 0))
    prev = pl.BlockSpec((1, CONV_HALO, GDN_CONV_CH), lambda bi, i: (bi, jnp.maximum(i * hb - 1, 0), 0))
    nxt = pl.BlockSpec((1, CONV_HALO, GDN_CONV_CH),
                       lambda bi, i: (bi, jnp.minimum((i + 1) * hb, nh - 1), 0))
    return pl.pallas_call(
        _conv_kernel,
        grid=(b, nblk),
        in_specs=[prev, cur(GDN_CONV_CH), nxt, _const_spec(conv_w.shape), cur(GDN_GATE_LANES),
                  _const_spec((1, GDN_GATE_LANES)), _const_spec((1, GDN_GATE_LANES))],
        out_specs=[cur(GDN_CONV_CH),
                   pl.BlockSpec((2, 1, tb, GDN_GATE_LANES), lambda bi, i: (0, bi, i, 0))],
        out_shape=[jax.ShapeDtypeStruct((b, t, GDN_CONV_CH), BF16),
                   jax.ShapeDtypeStruct((2, b, t, GDN_GATE_LANES), F32)],
        scratch_shapes=[pltpu.VMEM((tb + 2 * CONV_HALO, GDN_CONV_CH), BF16)],
        compiler_params=_cparams(("parallel", "arbitrary")),
        name="conv_gates",
    )(qkv, qkv, qkv, conv_w, graw, alog_l, dtb_l)


def _split3(x):
    hi = x.astype(BF16)
    r1 = x - hi.astype(F32)
    mid = r1.astype(BF16)
    lo = (r1 - mid.astype(F32)).astype(BF16)
    return hi, mid, lo


def _dot(a, b):
    return jnp.dot(a, b, preferred_element_type=F32)


def _unit_triangular_inverses(a_list):
    c = a_list[0].shape[0]
    row = lax.broadcasted_iota(jnp.int32, (c, 2 * c), 0)
    col = lax.broadcasted_iota(jnp.int32, (c, 2 * c), 1)
    x_half = col >= c
    eye = jnp.where(col == row + c, 1.0, 0.0)
    zs = [jnp.where(x_half, eye, -jnp.concatenate([a, a], axis=1)) for a in a_list]
    for _ in range(int(np.log2(c))):
        zbs = [z.astype(BF16) for z in zs]
        zs = [_dot(zb[:, :c], zb) + jnp.where(x_half, z, 0.0) for z, zb in zip(zs, zbs)]
    return [z[:, c:] for z in zs]


def _delta_prepare(chunks, ins, wq_ref, u_ref, aqk_ref, kd_ref, glast_ref):
    c = GDN_CHUNK
    row = lax.broadcasted_iota(jnp.int32, (c, c), 0)
    col = lax.broadcasted_iota(jnp.int32, (c, c), 1)
    nt = (((1,), (1,)), ((), ()))
    chains = []
    for cc in chunks:
        rows = pl.ds(cc * c, c)
        for direction, (qkv_ref, gate_ref) in enumerate(ins):
            causal = (row >= col) if direction == 0 else (row <= col)
            strict = (row > col) if direction == 0 else (row < col)
            tri = jnp.where(causal, 1.0, 0.0).astype(BF16)
            gates = gate_ref[0, 0, rows, :]
            hi, mid, lo = _split3(gates)
            gcum = _dot(tri, hi) + _dot(tri, mid) + _dot(tri, lo)
            gcum_t = gcum.T
            gtot = jnp.sum(gates, axis=0, keepdims=True)
            glast_ref[cc, direction] = jnp.broadcast_to(jnp.exp(gtot), (8, GDN_GATE_LANES))
            for h in range(GDN_HEADS):
                gl = GDN_HEADS + h
                chains.append(dict(
                    slot=(cc, direction * GDN_HEADS + h), causal=causal, strict=strict,
                    beta=gates[:, h:h + 1], gc=gcum[:, gl:gl + 1], gr=gcum_t[gl:gl + 1, :],
                    g_last=gtot[:, gl:gl + 1],
                    q=qkv_ref[0, rows, h * GDN_DK:(h + 1) * GDN_DK],
                    k=qkv_ref[0, rows, GDN_KW + h * GDN_DK:GDN_KW + (h + 1) * GDN_DK],
                    v=qkv_ref[0, rows, 2 * GDN_KW + h * GDN_DV:2 * GDN_KW + (h + 1) * GDN_DV]))
    for ch in chains:
        ch["kf"] = ch["k"].astype(F32)
        ch["kbeta"] = ch["kf"] * ch["beta"]
    kqs = [lax.dot_general(jnp.concatenate([ch["kbeta"].astype(BF16), ch["q"]], axis=0), ch["k"], nt,
                           preferred_element_type=F32) for ch in chains]
    a_list = []
    for ch, kq in zip(chains, kqs):
        decay = jnp.where(ch["causal"], jnp.exp(jnp.where(ch["causal"], ch["gc"] - ch["gr"], 0.0)), 0.0)
        a_list.append(jnp.where(ch["strict"], kq[:c] * decay, 0.0))
        ch["aqk"] = (kq[c:] * decay).astype(BF16)
    tinvs = _unit_triangular_inverses(a_list)
    rhss = []
    for ch in chains:
        ch["eg"] = jnp.exp(ch["gc"])
        rhss.append(jnp.concatenate([(ch["kbeta"] * ch["eg"]).astype(BF16),
                                     (ch["v"].astype(F32) * ch["beta"]).astype(BF16)], axis=1))
    wus = [_dot(t.astype(BF16), rhs) for t, rhs in zip(tinvs, rhss)]
    for ch, wu in zip(chains, wus):
        cc, g = ch["slot"]
        wq_ref[cc, g, 0:c, :] = wu[:, :GDN_DK].astype(BF16)
        wq_ref[cc, g, c:2 * c, :] = (ch["q"].astype(F32) * ch["eg"]).astype(BF16)
        u_ref[cc, g] = wu[:, GDN_DK:]
        aqk_ref[cc, g] = ch["aqk"]
        kd_ref[cc, g] = (ch["kf"] * jnp.exp(ch["g_last"] - ch["gc"])).astype(BF16)


def _delta_kernel(qkvf_ref, gf_ref, qkvb_ref, gb_ref,
                  of_ref, ob_ref, s_ref, wq_ref, u_ref, aqk_ref, kd_ref, glast_ref):
    i = pl.program_id(1)
    c = GDN_CHUNK
    n_chunks = GDN_BLOCK_TOKENS // c
    ins = ((qkvf_ref, gf_ref), (qkvb_ref, gb_ref))
    outs = (of_ref, ob_ref)

    @pl.when(i == 0)
    def _():
        s_ref[...] = jnp.zeros_like(s_ref)

    for c0 in range(0, n_chunks, GDN_PREPARE_CHUNKS):
        _delta_prepare(range(c0, c0 + GDN_PREPARE_CHUNKS), ins, wq_ref, u_ref, aqk_ref, kd_ref, glast_ref)

    tn = (((0,), (0,)), ((), ()))
    n_chains = 2 * GDN_HEADS
    for ci, g0 in [(ci, g0) for ci in range(n_chunks) for g0 in range(0, n_chains, GDN_SCAN_CHAINS)]:
        group = range(g0, g0 + GDN_SCAN_CHAINS)
        chunk_of = {g: ci if g < GDN_HEADS else n_chunks - 1 - ci for g in group}
        wss = {g: _dot(wq_ref[chunk_of[g], g], s_ref[g].astype(BF16)) for g in group}
        vnbs = {g: (u_ref[chunk_of[g], g] - wss[g][:c]).astype(BF16) for g in group}
        for g in group:
            o = wss[g][c:] + _dot(aqk_ref[chunk_of[g], g], vnbs[g])
            h = g % GDN_HEADS
            outs[g // GDN_HEADS][0, pl.ds(chunk_of[g] * c, c), h * GDN_DV:(h + 1) * GDN_DV] = o.astype(BF16)
        for g in group:
            gl = GDN_HEADS + g % GDN_HEADS
            e_last = glast_ref[chunk_of[g], g // GDN_HEADS, 0:1, gl:gl + 1]
            s_ref[g] = s_ref[g] * e_last + lax.dot_general(kd_ref[chunk_of[g], g], vnbs[g], tn,
                                                           preferred_element_type=F32)


def _gated_delta(qkv, gates):
    b, t, _ = qkv.shape
    tb = GDN_BLOCK_TOKENS
    nblk = t // tb
    n_chunks = tb // GDN_CHUNK
    n_chains = 2 * GDN_HEADS
    fwd = lambda width: pl.BlockSpec((1, tb, width), lambda bi, i: (bi, i, 0))
    bwd = lambda width: pl.BlockSpec((1, tb, width), lambda bi, i: (bi, nblk - 1 - i, 0))
    gate_f = pl.BlockSpec((1, 1, tb, GDN_GATE_LANES), lambda bi, i: (0, bi, i, 0))
    gate_b = pl.BlockSpec((1, 1, tb, GDN_GATE_LANES), lambda bi, i: (1, bi, nblk - 1 - i, 0))
    out = jax.ShapeDtypeStruct((b, t, GDN_VW), BF16)
    return pl.pallas_call(
        _delta_kernel,
        grid=(b, nblk),
        in_specs=[fwd(GDN_CONV_CH), gate_f, bwd(GDN_CONV_CH), gate_b],
        out_specs=[fwd(GDN_VW), bwd(GDN_VW)],
        out_shape=[out, out],
        scratch_shapes=[pltpu.VMEM((n_chains, GDN_DK, GDN_DV), F32),
                        pltpu.VMEM((n_chunks, n_chains, 2 * GDN_CHUNK, GDN_DK), BF16),
                        pltpu.VMEM((n_chunks, n_chains, GDN_CHUNK, GDN_DV), F32),
                        pltpu.VMEM((n_chunks, n_chains, GDN_CHUNK, GDN_CHUNK), BF16),
                        pltpu.VMEM((n_chunks, n_chains, GDN_CHUNK, GDN_DK), BF16),
                        pltpu.VMEM((n_chunks, 2, 8, GDN_GATE_LANES), F32)],
        compiler_params=_cparams(("parallel", "arbitrary")),
        name="gated_delta",
    )(qkv, gates, qkv, gates)


def _out_kernel(of_ref, ob_ref, z_ref, h_ref, nw_ref, wo_ref, fw_ref, y_ref, og_ref):
    for hd in range(GDN_HEADS):
        lanes = slice(hd * GDN_DV, (hd + 1) * GDN_DV)
        o = of_ref[:, lanes].astype(F32) + ob_ref[:, lanes].astype(F32)
        z = z_ref[:, lanes].astype(F32)
        on = o * _rms_scale(o) * nw_ref[...]
        og_ref[:, lanes] = (on * _silu(z)).astype(og_ref.dtype)
    h = h_ref[...] + jnp.dot(og_ref[...], wo_ref[...], preferred_element_type=F32)
    y_ref[...] = h * _rms_scale(h) * fw_ref[...]


def _out_proj(of2, ob2, z2, h2, nw, wo, fw):
    n = h2.shape[0]
    tm = TOKEN_TILE
    tok = lambda width: pl.BlockSpec((tm, width), lambda i: (i, 0))
    return pl.pallas_call(
        _out_kernel,
        grid=(n // tm,),
        in_specs=[tok(GDN_VW), tok(GDN_VW), tok(GDN_VW), tok(D_MODEL),
                  _const_spec((1, GDN_DV)), _const_spec(wo.shape), _const_spec((1, D_MODEL))],
        out_specs=tok(D_MODEL),
        out_shape=jax.ShapeDtypeStruct((n, D_MODEL), F32),
        scratch_shapes=[pltpu.VMEM((tm, GDN_VW), BF16)],
        compiler_params=_cparams(("parallel",)),
        name="out_proj",
    )(of2, ob2, z2, h2, nw, wo, fw)


def _trunk(x, p):
    b, t, _ = x.shape
    n = b * t
    x2 = x.reshape(n, D_MODEL)
    qg, kv = _na_in_proj(x2, p["ln0"], p["na_w_in"])
    seq = lambda a: a.reshape(b, t, a.shape[-1])
    og = _neighbourhood_attention(seq(qg), seq(kv), p["na_bias"])
    h1, qkv, z, graw = _mid_proj(x2, og.reshape(n, NA_WIDTH), p["na_w_out"], p["ln1"],
                                 p["w_qkv"], p["w_z"], p["w_gate"])
    qkv_n, gates = _conv_gates(seq(qkv), seq(graw), p["conv_w"], p["alog_l"], p["dtb_l"])
    o_f, o_b = _gated_delta(qkv_n, gates)
    y = _out_proj(o_f.reshape(n, GDN_VW), o_b.reshape(n, GDN_VW), z, h1,
                  p["gdn_norm_w"], p["gdn_w_out"], p["final_w"])
    return y.reshape(b, t, D_MODEL)


def kernel(x_prompt, x_sample, ln_w, na_w_in, na_rpb, na_w_out, gdn_w_in, gdn_conv_w, gdn_a_log,
           gdn_dt_bias, gdn_norm_w, gdn_w_out, final_norm_w):
    o1 = GDN_CONV_CH
    o2 = o1 + GDN_VW
    w_in = gdn_w_in[0]
    n_gate = 4 * GDN_HEADS
    lane_pad = lambda a: jnp.pad(a, ((0, 0), (0, GDN_GATE_LANES - a.shape[1])))
    shift = lambda a: jnp.pad(a.reshape(1, 2 * GDN_HEADS).astype(F32),
                              ((0, 0), (2 * GDN_HEADS, GDN_GATE_LANES - n_gate)))
    p = {
        "ln0": ln_w[0].reshape(1, D_MODEL),
        "ln1": ln_w[1].reshape(1, D_MODEL),
        "na_w_in": na_w_in[0].astype(BF16),
        "na_w_out": na_w_out[0].astype(BF16),
        "w_qkv": w_in[:, :o1].astype(BF16),
        "w_z": w_in[:, o1:o2].astype(BF16),
        "w_gate": lane_pad(w_in[:, o2:]).astype(BF16),
        "conv_w": jnp.pad(gdn_conv_w[0], ((0, 8 - GDN_CONV), (0, 0))),
        "alog_l": shift(gdn_a_log[0]),
        "dtb_l": shift(gdn_dt_bias[0]),
        "gdn_norm_w": gdn_norm_w[0].reshape(1, GDN_DV),
        "gdn_w_out": gdn_w_out[0].astype(BF16),
        "final_w": final_norm_w.reshape(1, D_MODEL),
    }
    p["na_bias"] = _na_bias_tables(na_rpb[0])
    return tuple(_trunk(x, p) for x in (x_prompt, x_sample))
```

```python
import functools

import numpy as np
import jax
import jax.numpy as jnp
from jax import lax
from jax.experimental import pallas as pl
from jax.experimental.pallas import tpu as pltpu

F32 = jnp.float32
BF16 = jnp.bfloat16

D_MODEL = 1024
GRID_W = 64
RMS_EPS = 1e-6
L2_EPS = 1e-6
NEG_INF = -1e30

NA_HEADS = 16
NA_HEAD_DIM = 64
NA_WIDTH = NA_HEADS * NA_HEAD_DIM
NA_WIN_H = 8
NA_WIN_W = 16
NA_ROWS_PER_BLOCK = 4
NA_WIN_TOKENS = NA_WIN_H * GRID_W

GDN_HEADS = 8
GDN_DK = 128
GDN_DV = 256
GDN_KW = GDN_HEADS * GDN_DK
GDN_VW = GDN_HEADS * GDN_DV
GDN_CONV = 5
GDN_CONV_CH = 2 * GDN_KW + GDN_VW
GDN_CHUNK = 64
GDN_PREPARE_CHUNKS = 2
GDN_SCAN_CHAINS = 16
GDN_GATE_LANES = 128
GDN_BLOCK_TOKENS = 512
CONV_BLOCK_TOKENS = 512
CONV_HALO = 16
CONV_GROUP_ROWS = 64
CONV_LANE_CHUNK = 512

TOKEN_TILE = 512
VMEM_LIMIT = 56 * 1024 * 1024


def _cparams(sem):
    return pltpu.CompilerParams(dimension_semantics=sem, vmem_limit_bytes=VMEM_LIMIT)


def _rms_scale(x):
    return lax.rsqrt(jnp.mean(x * x, axis=-1, keepdims=True) + RMS_EPS)


def _silu(x):
    return x * jax.nn.sigmoid(x)


def _const_spec(shape):
    nd = len(shape)
    return pl.BlockSpec(shape, lambda *_: (0,) * nd, pipeline_mode=pl.Buffered(1))


def _na_in_kernel(x_ref, lnw_ref, w_ref, qg_ref, kv_ref):
    x = x_ref[...]
    xn = (x * _rms_scale(x) * lnw_ref[...]).astype(BF16)
    dests = ((qg_ref, 0), (kv_ref, 0), (kv_ref, NA_WIDTH), (qg_ref, NA_WIDTH))
    for j, (o_ref, col) in enumerate(dests):
        y = jnp.dot(xn, w_ref[:, j * NA_WIDTH:(j + 1) * NA_WIDTH], preferred_element_type=F32)
        if j == 0:
            y = y * (NA_HEAD_DIM ** -0.5)
        o_ref[:, col:col + NA_WIDTH] = y.astype(o_ref.dtype)


def _na_in_proj(x2, lnw, w):
    n = x2.shape[0]
    tm = TOKEN_TILE
    tok = lambda width: pl.BlockSpec((tm, width), lambda i: (i, 0))
    out = jax.ShapeDtypeStruct((n, 2 * NA_WIDTH), BF16)
    return pl.pallas_call(
        _na_in_kernel,
        grid=(n // tm,),
        in_specs=[tok(D_MODEL), _const_spec((1, D_MODEL)), _const_spec((D_MODEL, 4 * NA_WIDTH))],
        out_specs=[tok(2 * NA_WIDTH)] * 2,
        out_shape=[out] * 2,
        compiler_params=_cparams(("parallel",)),
        name="na_in_proj",
    )(x2, lnw, w)


def _na_bias_tables(rpb):
    qc = np.arange(GRID_W)[:, None]
    kc = np.arange(GRID_W)[None, :]
    win_start = np.clip(qc - NA_WIN_W // 2, 0, GRID_W - NA_WIN_W)
    col_ok = (kc >= win_start) & (kc < win_start + NA_WIN_W)
    side = GRID_W - NA_WIN_W
    padded = jnp.pad(rpb.astype(F32), ((0, 0), (0, 0), (side, side)))
    toeplitz = jnp.stack([padded[:, :, GRID_W - 1 - c:2 * GRID_W - 1 - c] for c in range(GRID_W)],
                         axis=2)
    toeplitz = jnp.where(col_ok[None, None], toeplitz, NEG_INF)
    rb = NA_ROWS_PER_BLOCK
    d_all = list(range(rb)) + [NA_WIN_H // 2] * rb + [NA_WIN_H - rb + j for j in range(rb)]
    tables = []
    for d in d_all:
        t = toeplitz[:, NA_WIN_H - 1 - d:2 * NA_WIN_H - 1 - d]
        tables.append(t.transpose(0, 2, 1, 3).reshape(NA_HEADS, GRID_W, NA_WIN_TOKENS))
    return jnp.stack(tables)


def _na_kernel(qg_ref, kvp_ref, kvc_ref, kvn_ref, bias_ref, o_ref, kvcat_ref, *, rows):
    rb = NA_ROWS_PER_BLOCK
    blk_tokens = rb * GRID_W
    i = pl.program_id(0)
    kvcat_ref[0:blk_tokens, :] = kvp_ref[0]
    kvcat_ref[blk_tokens:2 * blk_tokens, :] = kvc_ref[0]
    kvcat_ref[2 * blk_tokens:3 * blk_tokens, :] = kvn_ref[0]

    lane = lax.broadcasted_iota(jnp.int32, (GRID_W, 2 * NA_HEAD_DIM), 1)
    first_half = lane < NA_HEAD_DIM
    r0 = i * rb
    nt = (((1,), (1,)), ((), ()))
    pair_lanes = [pl.ds(p * 2 * NA_HEAD_DIM, 2 * NA_HEAD_DIM) for p in range(NA_HEADS // 2)]
    pair_lanes_hi = [pl.ds(NA_WIDTH + p * 2 * NA_HEAD_DIM, 2 * NA_HEAD_DIM) for p in range(NA_HEADS // 2)]

    def key_rows(qi):
        rs = jnp.clip(r0 + qi - NA_WIN_H // 2, 0, rows - NA_WIN_H)
        return pl.ds(pl.multiple_of((rs - (r0 - rb)) * GRID_W, GRID_W), NA_WIN_TOKENS)

    def scores(qi):
        qrows = pl.ds(qi * GRID_W, GRID_W)
        krows = key_rows(qi)
        ss = []
        for p, lanes in enumerate(pair_lanes):
            qp = qg_ref[0, qrows, lanes]
            kw = kvcat_ref[krows, lanes]
            zero = jnp.zeros_like(qp)
            q2 = jnp.concatenate([jnp.where(first_half, qp, zero), jnp.where(first_half, zero, qp)], axis=0)
            s2 = lax.dot_general(q2, kw, nt, preferred_element_type=F32)
            for half in range(2):
                ss.append(s2[half * GRID_W:(half + 1) * GRID_W] + bias_ref[qi, 2 * p + half])
        return ss

    pending = scores(0)
    for qi in range(rb):
        ss = pending
        if qi + 1 < rb:
            pending = scores(qi + 1)
        qrows = pl.ds(qi * GRID_W, GRID_W)
        krows = key_rows(qi)
        es, ls = [], []
        for s in ss:
            e = jnp.exp(s - jnp.max(s, axis=-1, keepdims=True))
            ls.append(jnp.sum(e, axis=-1, keepdims=True))
            es.append(e.astype(BF16))
        for p, lanes in enumerate(pair_lanes):
            vw = kvcat_ref[krows, pair_lanes_hi[p]]
            o2 = jnp.dot(jnp.concatenate(es[2 * p:2 * p + 2], axis=0), vw, preferred_element_type=F32)
            o_half = [o2[half * GRID_W:(half + 1) * GRID_W] / ls[2 * p + half] for half in range(2)]
            o = jnp.where(first_half, o_half[0], o_half[1])
            g = qg_ref[0, qrows, pair_lanes_hi[p]].astype(F32)
            o_ref[0, qrows, lanes] = (o * _silu(g)).astype(o_ref.dtype)


def _neighbourhood_attention(qg, kv, bias):
    b, t, _ = qg.shape
    rows = t // GRID_W
    rb = NA_ROWS_PER_BLOCK
    assert rows % rb == 0 and rows >= 2 * NA_WIN_H
    nb = rows // rb
    bt = rb * GRID_W
    cur = lambda width: pl.BlockSpec((1, bt, width), lambda i, bi: (bi, i, 0))
    prev = pl.BlockSpec((1, bt, 2 * NA_WIDTH), lambda i, bi: (bi, jnp.maximum(i - 1, 0), 0))
    nxt = pl.BlockSpec((1, bt, 2 * NA_WIDTH), lambda i, bi: (bi, jnp.minimum(i + 1, nb - 1), 0))
    case = lambda i: jnp.where(i == 0, 0, jnp.where(i == nb - 1, 2, 1))
    bias_spec = pl.BlockSpec((rb, NA_HEADS, GRID_W, NA_WIN_TOKENS), lambda i, bi: (case(i), 0, 0, 0))
    return pl.pallas_call(
        functools.partial(_na_kernel, rows=rows),
        grid=(nb, b),
        in_specs=[cur(2 * NA_WIDTH), prev, cur(2 * NA_WIDTH), nxt, bias_spec],
        out_specs=cur(NA_WIDTH),
        out_shape=jax.ShapeDtypeStruct((b, t, NA_WIDTH), BF16),
        scratch_shapes=[pltpu.VMEM((3 * bt, 2 * NA_WIDTH), BF16)],
        compiler_params=_cparams(("arbitrary", "arbitrary")),
        name="neighbourhood_attention",
    )(qg, kv, kv, kv, bias)


def _mid_kernel(x_ref, og_ref, wo_ref, lnw_ref, wqkv_ref, wz_ref, wg_ref,
                h_ref, qkv_ref, z_ref, gate_ref):
    h = x_ref[...] + jnp.dot(og_ref[...], wo_ref[...], preferred_element_type=F32)
    h_ref[...] = h
    hn = (h * _rms_scale(h) * lnw_ref[...]).astype(BF16)
    qkv_ref[...] = jnp.dot(hn, wqkv_ref[...], preferred_element_type=F32).astype(qkv_ref.dtype)
    z_ref[...] = jnp.dot(hn, wz_ref[...], preferred_element_type=F32).astype(z_ref.dtype)
    gate_ref[...] = jnp.dot(hn, wg_ref[...], preferred_element_type=F32)


def _mid_proj(x2, og2, wo, lnw, wqkv, wz, wg):
    n = x2.shape[0]
    tm = TOKEN_TILE // 2
    tok = lambda width: pl.BlockSpec((tm, width), lambda i: (i, 0))
    return pl.pallas_call(
        _mid_kernel,
        grid=(n // tm,),
        in_specs=[tok(D_MODEL), tok(NA_WIDTH), _const_spec(wo.shape), _const_spec((1, D_MODEL)),
                  _const_spec(wqkv.shape), _const_spec(wz.shape), _const_spec(wg.shape)],
        out_specs=[tok(D_MODEL), tok(GDN_CONV_CH), tok(GDN_VW), tok(GDN_GATE_LANES)],
        out_shape=[jax.ShapeDtypeStruct((n, D_MODEL), F32),
                   jax.ShapeDtypeStruct((n, GDN_CONV_CH), BF16),
                   jax.ShapeDtypeStruct((n, GDN_VW), BF16),
                   jax.ShapeDtypeStruct((n, GDN_GATE_LANES), F32)],
        compiler_params=_cparams(("parallel",)),
        name="mid_proj",
    )(x2, og2, wo, lnw, wqkv, wz, wg)


def _conv_kernel(prev_ref, cur_ref, next_ref, cw_ref, graw_ref, alog_ref, dtb_ref,
                 qkv_ref, gate_ref, ext_ref):
    i = pl.program_id(1)
    nblk = pl.num_programs(1)
    tb = CONV_BLOCK_TOKENS
    pad = GDN_CONV // 2
    halo = CONV_HALO
    grp = CONV_GROUP_ROWS
    ext_ref[0:halo, :] = jnp.where(i > 0, prev_ref[0], jnp.zeros_like(prev_ref[0]))
    ext_ref[halo:halo + tb, :] = cur_ref[0]
    ext_ref[halo + tb:, :] = jnp.where(i < nblk - 1, next_ref[0], jnp.zeros_like(next_ref[0]))

    taps = [j for j in range(GDN_CONV) if j != pad]
    srow = lax.broadcasted_iota(jnp.int32, (len(taps) * grp, grp + 2 * halo), 0)
    scol = lax.broadcasted_iota(jnp.int32, (len(taps) * grp, grp + 2 * halo), 1)
    tap_idx = srow // grp
    offset = jnp.where(tap_idx < pad, tap_idx - pad, tap_idx - pad + 1)
    shift = jnp.where(scol == halo + srow % grp + offset, 1.0, 0.0).astype(BF16)

    lane_chunk = CONV_LANE_CHUNK
    work = [(r, lc) for r in range(tb // grp) for lc in range(GDN_CONV_CH // lane_chunk)]

    def shifted(item):
        r, lc = item
        return _dot(shift, ext_ref[r * grp:r * grp + grp + 2 * halo, lc * lane_chunk:(lc + 1) * lane_chunk])

    pending = shifted(work[0])
    for n, (r, lc) in enumerate(work):
        y = pending
        if n + 1 < len(work):
            pending = shifted(work[n + 1])
        lanes = slice(lc * lane_chunk, (lc + 1) * lane_chunk)
        rows = slice(r * grp, (r + 1) * grp)
        acc = ext_ref[halo + r * grp:halo + (r + 1) * grp, lanes].astype(F32) * cw_ref[pad:pad + 1, lanes]
        for m, j in enumerate(taps):
            acc = acc + y[m * grp:(m + 1) * grp] * cw_ref[j:j + 1, lanes]
        act = _silu(acc)
        for sub in range(lane_chunk // GDN_DK):
            lo = lc * lane_chunk + sub * GDN_DK
            a = act[:, sub * GDN_DK:(sub + 1) * GDN_DK]
            if lo < 2 * GDN_KW:
                scale = GDN_DK ** -0.5 if lo < GDN_KW else 1.0
                a = a * (lax.rsqrt(jnp.sum(a * a, axis=-1, keepdims=True) + L2_EPS) * scale)
            qkv_ref[0, rows, lo:lo + GDN_DK] = a.astype(qkv_ref.dtype)

    x = graw_ref[0]
    lane = lax.broadcasted_iota(jnp.int32, x.shape, 1)
    beta = jax.nn.sigmoid(x)
    decay = -jnp.exp(alog_ref[...]) * jax.nn.softplus(x + dtb_ref[...])
    for d in range(2):
        bsh = beta if d == 0 else pltpu.roll(beta, GDN_GATE_LANES - d * GDN_HEADS, 1)
        gsh = pltpu.roll(decay, GDN_GATE_LANES - (1 + d) * GDN_HEADS, 1)
        gate_ref[d, 0] = jnp.where(lane < GDN_HEADS, bsh, jnp.where(lane < 2 * GDN_HEADS, gsh, 0.0))


def _conv_gates(qkv, graw, conv_w, alog_l, dtb_l):
    b, t, _ = qkv.shape
    tb = CONV_BLOCK_TOKENS
    nblk = t // tb
    hb = tb // CONV_HALO
    nh = t // CONV_HALO
    cur = lambda width: pl.BlockSpec((1, tb, width), lambda bi, i: (bi, i, 0))
    prev = pl.BlockSpec((1, CONV_HALO, GDN_CONV_CH), lambda bi, i: (bi, jnp.maximum(i * hb - 1, 0), 0))
    nxt = pl.BlockSpec((1, CONV_HALO, GDN_CONV_CH),
                       lambda bi, i: (bi, jnp.minimum((i + 1) * hb, nh - 1), 0))
    return pl.pallas_call(
        _conv_kernel,
        grid=(b, nblk),
        in_specs=[prev, cur(GDN_CONV_CH), nxt, _const_spec(conv_w.shape), cur(GDN_GATE_LANES),
                  _const_spec((1, GDN_GATE_LANES)), _const_spec((1, GDN_GATE_LANES))],
        out_specs=[cur(GDN_CONV_CH),
                   pl.BlockSpec((2, 1, tb, GDN_GATE_LANES), lambda bi, i: (0, bi, i, 0))],
        out_shape=[jax.ShapeDtypeStruct((b, t, GDN_CONV_CH), BF16),
                   jax.ShapeDtypeStruct((2, b, t, GDN_GATE_LANES), F32)],
        scratch_shapes=[pltpu.VMEM((tb + 2 * CONV_HALO, GDN_CONV_CH), BF16)],
        compiler_params=_cparams(("parallel", "arbitrary")),
        name="conv_gates",
    )(qkv, qkv, qkv, conv_w, graw, alog_l, dtb_l)


def _split3(x):
    hi = x.astype(BF16)
    r1 = x - hi.astype(F32)
    mid = r1.astype(BF16)
    lo = (r1 - mid.astype(F32)).astype(BF16)
    return hi, mid, lo


def _dot(a, b):
    return jnp.dot(a, b, preferred_element_type=F32)


def _unit_triangular_inverses(a_list):
    c = a_list[0].shape[0]
    row = lax.broadcasted_iota(jnp.int32, (c, 2 * c), 0)
    col = lax.broadcasted_iota(jnp.int32, (c, 2 * c), 1)
    x_half = col >= c
    eye = jnp.where(col == row + c, 1.0, 0.0)
    zs = [jnp.where(x_half, eye, -jnp.concatenate([a, a], axis=1)) for a in a_list]
    for _ in range(int(np.log2(c))):
        zbs = [z.astype(BF16) for z in zs]
        zs = [_dot(zb[:, :c], zb) + jnp.where(x_half, z, 0.0) for z, zb in zip(zs, zbs)]
    return [z[:, c:] for z in zs]


def _delta_prepare(chunks, ins, wq_ref, u_ref, aqk_ref, kd_ref, glast_ref):
    c = GDN_CHUNK
    row = lax.broadcasted_iota(jnp.int32, (c, c), 0)
    col = lax.broadcasted_iota(jnp.int32, (c, c), 1)
    nt = (((1,), (1,)), ((), ()))
    chains = []
    for cc in chunks:
        rows = pl.ds(cc * c, c)
        for direction, (qkv_ref, gate_ref) in enumerate(ins):
            causal = (row >= col) if direction == 0 else (row <= col)
            strict = (row > col) if direction == 0 else (row < col)
            tri = jnp.where(causal, 1.0, 0.0).astype(BF16)
            gates = gate_ref[0, 0, rows, :]
            hi, mid, lo = _split3(gates)
            gcum = _dot(tri, hi) + _dot(tri, mid) + _dot(tri, lo)
            gcum_t = gcum.T
            gtot = jnp.sum(gates, axis=0, keepdims=True)
            glast_ref[cc, direction] = jnp.broadcast_to(jnp.exp(gtot), (8, GDN_GATE_LANES))
            for h in range(GDN_HEADS):
                gl = GDN_HEADS + h
                chains.append(dict(
                    slot=(cc, direction * GDN_HEADS + h), causal=causal, strict=strict,
                    beta=gates[:, h:h + 1], gc=gcum[:, gl:gl + 1], gr=gcum_t[gl:gl + 1, :],
                    g_last=gtot[:, gl:gl + 1],
                    q=qkv_ref[0, rows, h * GDN_DK:(h + 1) * GDN_DK],
                    k=qkv_ref[0, rows, GDN_KW + h * GDN_DK:GDN_KW + (h + 1) * GDN_DK],
                    v=qkv_ref[0, rows, 2 * GDN_KW + h * GDN_DV:2 * GDN_KW + (h + 1) * GDN_DV]))
    for ch in chains:
        ch["kf"] = ch["k"].astype(F32)
        ch["kbeta"] = ch["kf"] * ch["beta"]
    kqs = [lax.dot_general(jnp.concatenate([ch["kbeta"].astype(BF16), ch["q"]], axis=0), ch["k"], nt,
                           preferred_element_type=F32) for ch in chains]
    a_list = []
    for ch, kq in zip(chains, kqs):
        decay = jnp.where(ch["causal"], jnp.exp(jnp.where(ch["causal"], ch["gc"] - ch["gr"], 0.0)), 0.0)
        a_list.append(jnp.where(ch["strict"], kq[:c] * decay, 0.0))
        ch["aqk"] = (kq[c:] * decay).astype(BF16)
    tinvs = _unit_triangular_inverses(a_list)
    rhss = []
    for ch in chains:
        ch["eg"] = jnp.exp(ch["gc"])
        rhss.append(jnp.concatenate([(ch["kbeta"] * ch["eg"]).astype(BF16),
                                     (ch["v"].astype(F32) * ch["beta"]).astype(BF16)], axis=1))
    wus = [_dot(t.astype(BF16), rhs) for t, rhs in zip(tinvs, rhss)]
    for ch, wu in zip(chains, wus):
        cc, g = ch["slot"]
        wq_ref[cc, g, 0:c, :] = wu[:, :GDN_DK].astype(BF16)
        wq_ref[cc, g, c:2 * c, :] = (ch["q"].astype(F32) * ch["eg"]).astype(BF16)
        u_ref[cc, g] = wu[:, GDN_DK:]
        aqk_ref[cc, g] = ch["aqk"]
        kd_ref[cc, g] = (ch["kf"] * jnp.exp(ch["g_last"] - ch["gc"])).astype(BF16)


def _delta_kernel(qkvf_ref, gf_ref, qkvb_ref, gb_ref,
                  of_ref, ob_ref, s_ref, wq_ref, u_ref, aqk_ref, kd_ref, glast_ref):
    i = pl.program_id(1)
    c = GDN_CHUNK
    n_chunks = GDN_BLOCK_TOKENS // c
    ins = ((qkvf_ref, gf_ref), (qkvb_ref, gb_ref))
    outs = (of_ref, ob_ref)

    @pl.when(i == 0)
    def _():
        s_ref[...] = jnp.zeros_like(s_ref)

    for c0 in range(0, n_chunks, GDN_PREPARE_CHUNKS):
        _delta_prepare(range(c0, c0 + GDN_PREPARE_CHUNKS), ins, wq_ref, u_ref, aqk_ref, kd_ref, glast_ref)

    tn = (((0,), (0,)), ((), ()))
    n_chains = 2 * GDN_HEADS
    for ci, g0 in [(ci, g0) for ci in range(n_chunks) for g0 in range(0, n_chains, GDN_SCAN_CHAINS)]:
        group = range(g0, g0 + GDN_SCAN_CHAINS)
        chunk_of = {g: ci if g < GDN_HEADS else n_chunks - 1 - ci for g in group}
        wss = {g: _dot(wq_ref[chunk_of[g], g], s_ref[g].astype(BF16)) for g in group}
        vnbs = {g: (u_ref[chunk_of[g], g] - wss[g][:c]).astype(BF16) for g in group}
        for g in group:
            o = wss[g][c:] + _dot(aqk_ref[chunk_of[g], g], vnbs[g])
            h = g % GDN_HEADS
            outs[g // GDN_HEADS][0, pl.ds(chunk_of[g] * c, c), h * GDN_DV:(h + 1) * GDN_DV] = o.astype(BF16)
        for g in group:
            gl = GDN_HEADS + g % GDN_HEADS
            e_last = glast_ref[chunk_of[g], g // GDN_HEADS, 0:1, gl:gl + 1]
            s_ref[g] = s_ref[g] * e_last + lax.dot_general(kd_ref[chunk_of[g], g], vnbs[g], tn,
                                                           preferred_element_type=F32)


def _gated_delta(qkv, gates):
    b, t, _ = qkv.shape
    tb = GDN_BLOCK_TOKENS
    nblk = t // tb
    n_chunks = tb // GDN_CHUNK
    n_chains = 2 * GDN_HEADS
    fwd = lambda width: pl.BlockSpec((1, tb, width), lambda bi, i: (bi, i, 0))
    bwd = lambda width: pl.BlockSpec((1, tb, width), lambda bi, i: (bi, nblk - 1 - i, 0))
    gate_f = pl.BlockSpec((1, 1, tb, GDN_GATE_LANES), lambda bi, i: (0, bi, i, 0))
    gate_b = pl.BlockSpec((1, 1, tb, GDN_GATE_LANES), lambda bi, i: (1, bi, nblk - 1 - i, 0))
    out = jax.ShapeDtypeStruct((b, t, GDN_VW), BF16)
    return pl.pallas_call(
        _delta_kernel,
        grid=(b, nblk),
        in_specs=[fwd(GDN_CONV_CH), gate_f, bwd(GDN_CONV_CH), gate_b],
        out_specs=[fwd(GDN_VW), bwd(GDN_VW)],
        out_shape=[out, out],
        scratch_shapes=[pltpu.VMEM((n_chains, GDN_DK, GDN_DV), F32),
                        pltpu.VMEM((n_chunks, n_chains, 2 * GDN_CHUNK, GDN_DK), BF16),
                        pltpu.VMEM((n_chunks, n_chains, GDN_CHUNK, GDN_DV), F32),
                        pltpu.VMEM((n_chunks, n_chains, GDN_CHUNK, GDN_CHUNK), BF16),
                        pltpu.VMEM((n_chunks, n_chains, GDN_CHUNK, GDN_DK), BF16),
                        pltpu.VMEM((n_chunks, 2, 8, GDN_GATE_LANES), F32)],
        compiler_params=_cparams(("parallel", "arbitrary")),
        name="gated_delta",
    )(qkv, gates, qkv, gates)


def _out_kernel(of_ref, ob_ref, z_ref, h_ref, nw_ref, wo_ref, fw_ref, y_ref):
    def gated(hd):
        lanes = slice(hd * GDN_DV, (hd + 1) * GDN_DV)
        o = of_ref[:, lanes].astype(F32) + ob_ref[:, lanes].astype(F32)
        z = z_ref[:, lanes].astype(F32)
        on = o * _rms_scale(o) * nw_ref[...]
        return (on * _silu(z)).astype(BF16)

    h = h_ref[...]
    for hd in range(GDN_HEADS):
        h = h + _dot(gated(hd), wo_ref[hd * GDN_DV:(hd + 1) * GDN_DV, :])
    y_ref[...] = h * _rms_scale(h) * fw_ref[...]


def _out_proj(of2, ob2, z2, h2, nw, wo, fw):
    n = h2.shape[0]
    tm = TOKEN_TILE
    tok = lambda width: pl.BlockSpec((tm, width), lambda i: (i, 0))
    return pl.pallas_call(
        _out_kernel,
        grid=(n // tm,),
        in_specs=[tok(GDN_VW), tok(GDN_VW), tok(GDN_VW), tok(D_MODEL),
                  _const_spec((1, GDN_DV)), _const_spec(wo.shape), _const_spec((1, D_MODEL))],
        out_specs=tok(D_MODEL),
        out_shape=jax.ShapeDtypeStruct((n, D_MODEL), F32),
        compiler_params=_cparams(("parallel",)),
        name="out_proj",
    )(of2, ob2, z2, h2, nw, wo, fw)


def _trunk(x, p):
    b, t, _ = x.shape
    n = b * t
    x2 = x.reshape(n, D_MODEL)
    qg, kv = _na_in_proj(x2, p["ln0"], p["na_w_in"])
    seq = lambda a: a.reshape(b, t, a.shape[-1])
    og = _neighbourhood_attention(seq(qg), seq(kv), p["na_bias"])
    h1, qkv, z, graw = _mid_proj(x2, og.reshape(n, NA_WIDTH), p["na_w_out"], p["ln1"],
                                 p["w_qkv"], p["w_z"], p["w_gate"])
    qkv_n, gates = _conv_gates(seq(qkv), seq(graw), p["conv_w"], p["alog_l"], p["dtb_l"])
    o_f, o_b = _gated_delta(qkv_n, gates)
    y = _out_proj(o_f.reshape(n, GDN_VW), o_b.reshape(n, GDN_VW), z, h1,
                  p["gdn_norm_w"], p["gdn_w_out"], p["final_w"])
    return y.reshape(b, t, D_MODEL)


def kernel(x_prompt, x_sample, ln_w, na_w_in, na_rpb, na_w_out, gdn_w_in, gdn_conv_w, gdn_a_log,
           gdn_dt_bias, gdn_norm_w, gdn_w_out, final_norm_w):
    o1 = GDN_CONV_CH
    o2 = o1 + GDN_VW
    w_in = gdn_w_in[0]
    n_gate = 4 * GDN_HEADS
    lane_pad = lambda a: jnp.pad(a, ((0, 0), (0, GDN_GATE_LANES - a.shape[1])))
    shift = lambda a: jnp.pad(a.reshape(1, 2 * GDN_HEADS).astype(F32),
                              ((0, 0), (2 * GDN_HEADS, GDN_GATE_LANES - n_gate)))
    p = {
        "ln0": ln_w[0].reshape(1, D_MODEL),
        "ln1": ln_w[1].reshape(1, D_MODEL),
        "na_w_in": na_w_in[0].astype(BF16),
        "na_w_out": na_w_out[0].astype(BF16),
        "w_qkv": w_in[:, :o1].astype(BF16),
        "w_z": w_in[:, o1:o2].astype(BF16),
        "w_gate": lane_pad(w_in[:, o2:]).astype(BF16),
        "conv_w": jnp.pad(gdn_conv_w[0], ((0, 8 - GDN_CONV), (0, 0))),
        "alog_l": shift(gdn_a_log[0]),
        "dtb_l": shift(gdn_dt_bias[0]),
        "gdn_norm_w": gdn_norm_w[0].reshape(1, GDN_DV),
        "gdn_w_out": gdn_w_out[0].astype(BF16),
        "final_w": final_norm_w.reshape(1, D_MODEL),
    }
    p["na_bias"] = _na_bias_tables(na_rpb[0])
    return tuple(_trunk(x, p) for x in (x_prompt, x_sample))
```

```python
import functools

import numpy as np
import jax
import jax.numpy as jnp
from jax import lax
from jax.experimental import pallas as pl
from jax.experimental.pallas import tpu as pltpu

F32 = jnp.float32
BF16 = jnp.bfloat16

D_MODEL = 1024
GRID_W = 64
RMS_EPS = 1e-6
L2_EPS = 1e-6
NEG_INF = -1e30

NA_HEADS = 16
NA_HEAD_DIM = 64
NA_WIDTH = NA_HEADS * NA_HEAD_DIM
NA_WIN_H = 8
NA_WIN_W = 16
NA_ROWS_PER_BLOCK = 4
NA_WIN_TOKENS = NA_WIN_H * GRID_W
NA_PAIR_GROUP = 8

GDN_HEADS = 8
GDN_DK = 128
GDN_DV = 256
GDN_KW = GDN_HEADS * GDN_DK
GDN_VW = GDN_HEADS * GDN_DV
GDN_CONV = 5
GDN_CONV_CH = 2 * GDN_KW + GDN_VW
GDN_CHUNK = 64
GDN_PREPARE_CHUNKS = 2
GDN_SCAN_CHAINS = 16
GDN_GATE_LANES = 128
GDN_BLOCK_TOKENS = 512
CONV_BLOCK_TOKENS = 512
CONV_HALO = 16
CONV_GROUP_ROWS = 64
CONV_LANE_CHUNK = 512

TOKEN_TILE = 512
VMEM_LIMIT = 56 * 1024 * 1024


def _cparams(sem):
    return pltpu.CompilerParams(dimension_semantics=sem, vmem_limit_bytes=VMEM_LIMIT)


def _rms_scale(x):
    return lax.rsqrt(jnp.mean(x * x, axis=-1, keepdims=True) + RMS_EPS)


def _silu(x):
    return x * jax.nn.sigmoid(x)


def _const_spec(shape):
    nd = len(shape)
    return pl.BlockSpec(shape, lambda *_: (0,) * nd, pipeline_mode=pl.Buffered(1))


def _na_in_kernel(x_ref, lnw_ref, w_ref, qg_ref, kv_ref):
    x = x_ref[...]
    xn = (x * _rms_scale(x) * lnw_ref[...]).astype(BF16)
    dests = ((qg_ref, 0), (kv_ref, 0), (kv_ref, NA_WIDTH), (qg_ref, NA_WIDTH))
    for j, (o_ref, col) in enumerate(dests):
        y = jnp.dot(xn, w_ref[:, j * NA_WIDTH:(j + 1) * NA_WIDTH], preferred_element_type=F32)
        if j == 0:
            y = y * (NA_HEAD_DIM ** -0.5)
        o_ref[:, col:col + NA_WIDTH] = y.astype(o_ref.dtype)


def _na_in_proj(x2, lnw, w):
    n = x2.shape[0]
    tm = TOKEN_TILE
    tok = lambda width: pl.BlockSpec((tm, width), lambda i: (i, 0))
    out = jax.ShapeDtypeStruct((n, 2 * NA_WIDTH), BF16)
    return pl.pallas_call(
        _na_in_kernel,
        grid=(n // tm,),
        in_specs=[tok(D_MODEL), _const_spec((1, D_MODEL)), _const_spec((D_MODEL, 4 * NA_WIDTH))],
        out_specs=[tok(2 * NA_WIDTH)] * 2,
        out_shape=[out] * 2,
        compiler_params=_cparams(("parallel",)),
        name="na_in_proj",
    )(x2, lnw, w)


def _na_bias_tables(rpb):
    qc = np.arange(GRID_W)[:, None]
    kc = np.arange(GRID_W)[None, :]
    win_start = np.clip(qc - NA_WIN_W // 2, 0, GRID_W - NA_WIN_W)
    col_ok = (kc >= win_start) & (kc < win_start + NA_WIN_W)
    side = GRID_W - NA_WIN_W
    padded = jnp.pad(rpb.astype(F32), ((0, 0), (0, 0), (side, side)))
    toeplitz = jnp.stack([padded[:, :, GRID_W - 1 - c:2 * GRID_W - 1 - c] for c in range(GRID_W)],
                         axis=2)
    toeplitz = jnp.where(col_ok[None, None], toeplitz, NEG_INF)
    rb = NA_ROWS_PER_BLOCK
    d_all = list(range(rb)) + [NA_WIN_H // 2] * rb + [NA_WIN_H - rb + j for j in range(rb)]
    tables = []
    for d in d_all:
        t = toeplitz[:, NA_WIN_H - 1 - d:2 * NA_WIN_H - 1 - d]
        tables.append(t.transpose(0, 2, 1, 3).reshape(NA_HEADS, GRID_W, NA_WIN_TOKENS))
    return jnp.stack(tables)


def _na_kernel(qg_ref, kvp_ref, kvc_ref, kvn_ref, bias_ref, o_ref, kvcat_ref, *, rows):
    rb = NA_ROWS_PER_BLOCK
    blk_tokens = rb * GRID_W
    i = pl.program_id(0)
    kvcat_ref[0:blk_tokens, :] = kvp_ref[0]
    kvcat_ref[blk_tokens:2 * blk_tokens, :] = kvc_ref[0]
    kvcat_ref[2 * blk_tokens:3 * blk_tokens, :] = kvn_ref[0]

    lane = lax.broadcasted_iota(jnp.int32, (GRID_W, 2 * NA_HEAD_DIM), 1)
    first_half = lane < NA_HEAD_DIM
    r0 = i * rb
    nt = (((1,), (1,)), ((), ()))
    pair_lanes = [pl.ds(p * 2 * NA_HEAD_DIM, 2 * NA_HEAD_DIM) for p in range(NA_HEADS // 2)]
    pair_lanes_hi = [pl.ds(NA_WIDTH + p * 2 * NA_HEAD_DIM, 2 * NA_HEAD_DIM) for p in range(NA_HEADS // 2)]

    def key_rows(qi):
        rs = jnp.clip(r0 + qi - NA_WIN_H // 2, 0, rows - NA_WIN_H)
        return pl.ds(pl.multiple_of((rs - (r0 - rb)) * GRID_W, GRID_W), NA_WIN_TOKENS)

    for qi, p0 in [(qi, p0) for qi in range(rb) for p0 in range(0, NA_HEADS // 2, NA_PAIR_GROUP)]:
        pairs = range(p0, p0 + NA_PAIR_GROUP)
        qrows = pl.ds(qi * GRID_W, GRID_W)
        krows = key_rows(qi)
        ss = {}
        for p in pairs:
            qp = qg_ref[0, qrows, pair_lanes[p]]
            kw = kvcat_ref[krows, pair_lanes[p]]
            zero = jnp.zeros_like(qp)
            q2 = jnp.concatenate([jnp.where(first_half, qp, zero), jnp.where(first_half, zero, qp)], axis=0)
            s2 = lax.dot_general(q2, kw, nt, preferred_element_type=F32)
            for half in range(2):
                ss[2 * p + half] = s2[half * GRID_W:(half + 1) * GRID_W] + bias_ref[qi, 2 * p + half]
        es, ls = {}, {}
        for hd, s in ss.items():
            e = jnp.exp(s - jnp.max(s, axis=-1, keepdims=True))
            ls[hd] = jnp.sum(e, axis=-1, keepdims=True)
            es[hd] = e.astype(BF16)
        for p in pairs:
            vw = kvcat_ref[krows, pair_lanes_hi[p]]
            o2 = jnp.dot(jnp.concatenate([es[2 * p], es[2 * p + 1]], axis=0), vw, preferred_element_type=F32)
            o_half = [o2[half * GRID_W:(half + 1) * GRID_W] / ls[2 * p + half] for half in range(2)]
            o = jnp.where(first_half, o_half[0], o_half[1])
            g = qg_ref[0, qrows, pair_lanes_hi[p]].astype(F32)
            o_ref[0, qrows, pair_lanes[p]] = (o * _silu(g)).astype(o_ref.dtype)


def _neighbourhood_attention(qg, kv, bias):
    b, t, _ = qg.shape
    rows = t // GRID_W
    rb = NA_ROWS_PER_BLOCK
    assert rows % rb == 0 and rows >= 2 * NA_WIN_H
    nb = rows // rb
    bt = rb * GRID_W
    cur = lambda width: pl.BlockSpec((1, bt, width), lambda i, bi: (bi, i, 0))
    prev = pl.BlockSpec((1, bt, 2 * NA_WIDTH), lambda i, bi: (bi, jnp.maximum(i - 1, 0), 0))
    nxt = pl.BlockSpec((1, bt, 2 * NA_WIDTH), lambda i, bi: (bi, jnp.minimum(i + 1, nb - 1), 0))
    case = lambda i: jnp.where(i == 0, 0, jnp.where(i == nb - 1, 2, 1))
    bias_spec = pl.BlockSpec((rb, NA_HEADS, GRID_W, NA_WIN_TOKENS), lambda i, bi: (case(i), 0, 0, 0))
    return pl.pallas_call(
        functools.partial(_na_kernel, rows=rows),
        grid=(nb, b),
        in_specs=[cur(2 * NA_WIDTH), prev, cur(2 * NA_WIDTH), nxt, bias_spec],
        out_specs=cur(NA_WIDTH),
        out_shape=jax.ShapeDtypeStruct((b, t, NA_WIDTH), BF16),
        scratch_shapes=[pltpu.VMEM((3 * bt, 2 * NA_WIDTH), BF16)],
        compiler_params=_cparams(("arbitrary", "arbitrary")),
        name="neighbourhood_attention",
    )(qg, kv, kv, kv, bias)


def _mid_kernel(x_ref, og_ref, wo_ref, lnw_ref, wqkv_ref, wz_ref, wg_ref,
                h_ref, qkv_ref, z_ref, gate_ref):
    h = x_ref[...] + jnp.dot(og_ref[...], wo_ref[...], preferred_element_type=F32)
    h_ref[...] = h
    hn = (h * _rms_scale(h) * lnw_ref[...]).astype(BF16)
    qkv_ref[...] = jnp.dot(hn, wqkv_ref[...], preferred_element_type=F32).astype(qkv_ref.dtype)
    z_ref[...] = jnp.dot(hn, wz_ref[...], preferred_element_type=F32).astype(z_ref.dtype)
    gate_ref[...] = jnp.dot(hn, wg_ref[...], preferred_element_type=F32)


def _mid_proj(x2, og2, wo, lnw, wqkv, wz, wg):
    n = x2.shape[0]
    tm = TOKEN_TILE // 2
    tok = lambda width: pl.BlockSpec((tm, width), lambda i: (i, 0))
    return pl.pallas_call(
        _mid_kernel,
        grid=(n // tm,),
        in_specs=[tok(D_MODEL), tok(NA_WIDTH), _const_spec(wo.shape), _const_spec((1, D_MODEL)),
                  _const_spec(wqkv.shape), _const_spec(wz.shape), _const_spec(wg.shape)],
        out_specs=[tok(D_MODEL), tok(GDN_CONV_CH), tok(GDN_VW), tok(GDN_GATE_LANES)],
        out_shape=[jax.ShapeDtypeStruct((n, D_MODEL), F32),
                   jax.ShapeDtypeStruct((n, GDN_CONV_CH), BF16),
                   jax.ShapeDtypeStruct((n, GDN_VW), BF16),
                   jax.ShapeDtypeStruct((n, GDN_GATE_LANES), F32)],
        compiler_params=_cparams(("parallel",)),
        name="mid_proj",
    )(x2, og2, wo, lnw, wqkv, wz, wg)


def _conv_kernel(prev_ref, cur_ref, next_ref, cw_ref, graw_ref, alog_ref, dtb_ref,
                 qkv_ref, gate_ref, ext_ref):
    i = pl.program_id(1)
    nblk = pl.num_programs(1)
    tb = CONV_BLOCK_TOKENS
    pad = GDN_CONV // 2
    halo = CONV_HALO
    grp = CONV_GROUP_ROWS
    ext_ref[0:halo, :] = jnp.where(i > 0, prev_ref[0], jnp.zeros_like(prev_ref[0]))
    ext_ref[halo:halo + tb, :] = cur_ref[0]
    ext_ref[halo + tb:, :] = jnp.where(i < nblk - 1, next_ref[0], jnp.zeros_like(next_ref[0]))

    taps = [j for j in range(GDN_CONV) if j != pad]
    srow = lax.broadcasted_iota(jnp.int32, (len(taps) * grp, grp + 2 * halo), 0)
    scol = lax.broadcasted_iota(jnp.int32, (len(taps) * grp, grp + 2 * halo), 1)
    tap_idx = srow // grp
    offset = jnp.where(tap_idx < pad, tap_idx - pad, tap_idx - pad + 1)
    shift = jnp.where(scol == halo + srow % grp + offset, 1.0, 0.0).astype(BF16)

    lane_chunk = CONV_LANE_CHUNK
    work = [(r, lc) for r in range(tb // grp) for lc in range(GDN_CONV_CH // lane_chunk)]

    def shifted(item):
        r, lc = item
        return _dot(shift, ext_ref[r * grp:r * grp + grp + 2 * halo, lc * lane_chunk:(lc + 1) * lane_chunk])

    pending = shifted(work[0])
    for n, (r, lc) in enumerate(work):
        y = pending
        if n + 1 < len(work):
            pending = shifted(work[n + 1])
        lanes = slice(lc * lane_chunk, (lc + 1) * lane_chunk)
        rows = slice(r * grp, (r + 1) * grp)
        acc = ext_ref[halo + r * grp:halo + (r + 1) * grp, lanes].astype(F32) * cw_ref[pad:pad + 1, lanes]
        for m, j in enumerate(taps):
            acc = acc + y[m * grp:(m + 1) * grp] * cw_ref[j:j + 1, lanes]
        act = _silu(acc)
        for sub in range(lane_chunk // GDN_DK):
            lo = lc * lane_chunk + sub * GDN_DK
            a = act[:, sub * GDN_DK:(sub + 1) * GDN_DK]
            if lo < 2 * GDN_KW:
                scale = GDN_DK ** -0.5 if lo < GDN_KW else 1.0
                a = a * (lax.rsqrt(jnp.sum(a * a, axis=-1, keepdims=True) + L2_EPS) * scale)
            qkv_ref[0, rows, lo:lo + GDN_DK] = a.astype(qkv_ref.dtype)

    x = graw_ref[0]
    lane = lax.broadcasted_iota(jnp.int32, x.shape, 1)
    beta = jax.nn.sigmoid(x)
    decay = -jnp.exp(alog_ref[...]) * jax.nn.softplus(x + dtb_ref[...])
    for d in range(2):
        bsh = beta if d == 0 else pltpu.roll(beta, GDN_GATE_LANES - d * GDN_HEADS, 1)
        gsh = pltpu.roll(decay, GDN_GATE_LANES - (1 + d) * GDN_HEADS, 1)
        gate_ref[d, 0] = jnp.where(lane < GDN_HEADS, bsh, jnp.where(lane < 2 * GDN_HEADS, gsh, 0.0))


def _conv_gates(qkv, graw, conv_w, alog_l, dtb_l):
    b, t, _ = qkv.shape
    tb = CONV_BLOCK_TOKENS
    nblk = t // tb
    hb = tb // CONV_HALO
    nh = t // CONV_HALO
    cur = lambda width: pl.BlockSpec((1, tb, width), lambda bi, i: (bi, i, 0))
    prev = pl.BlockSpec((1, CONV_HALO, GDN_CONV_CH), lambda bi, i: (bi, jnp.maximum(i * hb - 1, 0), 0))
    nxt = pl.BlockSpec((1, CONV_HALO, GDN_CONV_CH),
                       lambda bi, i: (bi, jnp.minimum((i + 1) * hb, nh - 1), 0))
    return pl.pallas_call(
        _conv_kernel,
        grid=(b, nblk),
        in_specs=[prev, cur(GDN_CONV_CH), nxt, _const_spec(conv_w.shape), cur(GDN_GATE_LANES),
                  _const_spec((1, GDN_GATE_LANES)), _const_spec((1, GDN_GATE_LANES))],
        out_specs=[cur(GDN_CONV_CH),
                   pl.BlockSpec((2, 1, tb, GDN_GATE_LANES), lambda bi, i: (0, bi, i, 0))],
        out_shape=[jax.ShapeDtypeStruct((b, t, GDN_CONV_CH), BF16),
                   jax.ShapeDtypeStruct((2, b, t, GDN_GATE_LANES), F32)],
        scratch_shapes=[pltpu.VMEM((tb + 2 * CONV_HALO, GDN_CONV_CH), BF16)],
        compiler_params=_cparams(("parallel", "arbitrary")),
        name="conv_gates",
    )(qkv, qkv, qkv, conv_w, graw, alog_l, dtb_l)


def _split3(x):
    hi = x.astype(BF16)
    r1 = x - hi.astype(F32)
    mid = r1.astype(BF16)
    lo = (r1 - mid.astype(F32)).astype(BF16)
    return hi, mid, lo


def _dot(a, b):
    return jnp.dot(a, b, preferred_element_type=F32)


def _unit_triangular_inverses(a_list):
    c = a_list[0].shape[0]
    row = lax.broadcasted_iota(jnp.int32, (c, 2 * c), 0)
    col = lax.broadcasted_iota(jnp.int32, (c, 2 * c), 1)
    x_half = col >= c
    eye = jnp.where(col == row + c, 1.0, 0.0)
    zs = [jnp.where(x_half, eye, -jnp.concatenate([a, a], axis=1)) for a in a_list]
    for _ in range(int(np.log2(c))):
        zbs = [z.astype(BF16) for z in zs]
        zs = [_dot(zb[:, :c], zb) + jnp.where(x_half, z, 0.0) for z, zb in zip(zs, zbs)]
    return [z[:, c:] for z in zs]


def _delta_prepare(chunks, ins, wq_ref, u_ref, aqk_ref, kd_ref, glast_ref):
    c = GDN_CHUNK
    row = lax.broadcasted_iota(jnp.int32, (c, c), 0)
    col = lax.broadcasted_iota(jnp.int32, (c, c), 1)
    nt = (((1,), (1,)), ((), ()))
    chains = []
    for cc in chunks:
        rows = pl.ds(cc * c, c)
        for direction, (qkv_ref, gate_ref) in enumerate(ins):
            causal = (row >= col) if direction == 0 else (row <= col)
            strict = (row > col) if direction == 0 else (row < col)
            tri = jnp.where(causal, 1.0, 0.0).astype(BF16)
            gates = gate_ref[0, 0, rows, :]
            hi, mid, lo = _split3(gates)
            gcum = _dot(tri, hi) + _dot(tri, mid) + _dot(tri, lo)
            gcum_t = gcum.T
            gtot = jnp.sum(gates, axis=0, keepdims=True)
            glast_ref[cc, direction] = jnp.broadcast_to(jnp.exp(gtot), (8, GDN_GATE_LANES))
            for h in range(GDN_HEADS):
                gl = GDN_HEADS + h
                chains.append(dict(
                    slot=(cc, direction * GDN_HEADS + h), causal=causal, strict=strict,
                    beta=gates[:, h:h + 1], gc=gcum[:, gl:gl + 1], gr=gcum_t[gl:gl + 1, :],
                    g_last=gtot[:, gl:gl + 1],
                    q=qkv_ref[0, rows, h * GDN_DK:(h + 1) * GDN_DK],
                    k=qkv_ref[0, rows, GDN_KW + h * GDN_DK:GDN_KW + (h + 1) * GDN_DK],
                    v=qkv_ref[0, rows, 2 * GDN_KW + h * GDN_DV:2 * GDN_KW + (h + 1) * GDN_DV]))
    for ch in chains:
        ch["kf"] = ch["k"].astype(F32)
        ch["kbeta"] = ch["kf"] * ch["beta"]
    kqs = [lax.dot_general(jnp.concatenate([ch["kbeta"].astype(BF16), ch["q"]], axis=0), ch["k"], nt,
                           preferred_element_type=F32) for ch in chains]
    a_list = []
    for ch, kq in zip(chains, kqs):
        decay = jnp.where(ch["causal"], jnp.exp(jnp.where(ch["causal"], ch["gc"] - ch["gr"], 0.0)), 0.0)
        a_list.append(jnp.where(ch["strict"], kq[:c] * decay, 0.0))
        ch["aqk"] = (kq[c:] * decay).astype(BF16)
    tinvs = _unit_triangular_inverses(a_list)
    rhss = []
    for ch in chains:
        ch["eg"] = jnp.exp(ch["gc"])
        rhss.append(jnp.concatenate([(ch["kbeta"] * ch["eg"]).astype(BF16),
                                     (ch["v"].astype(F32) * ch["beta"]).astype(BF16)], axis=1))
    wus = [_dot(t.astype(BF16), rhs) for t, rhs in zip(tinvs, rhss)]
    for ch, wu in zip(chains, wus):
        cc, g = ch["slot"]
        wq_ref[cc, g, 0:c, :] = wu[:, :GDN_DK].astype(BF16)
        wq_ref[cc, g, c:2 * c, :] = (ch["q"].astype(F32) * ch["eg"]).astype(BF16)
        u_ref[cc, g] = wu[:, GDN_DK:]
        aqk_ref[cc, g] = ch["aqk"]
        kd_ref[cc, g] = (ch["kf"] * jnp.exp(ch["g_last"] - ch["gc"])).astype(BF16)


def _delta_kernel(qkvf_ref, gf_ref, qkvb_ref, gb_ref,
                  of_ref, ob_ref, s_ref, wq_ref, u_ref, aqk_ref, kd_ref, glast_ref):
    i = pl.program_id(1)
    c = GDN_CHUNK
    n_chunks = GDN_BLOCK_TOKENS // c
    ins = ((qkvf_ref, gf_ref), (qkvb_ref, gb_ref))
    outs = (of_ref, ob_ref)

    @pl.when(i == 0)
    def _():
        s_ref[...] = jnp.zeros_like(s_ref)

    for c0 in range(0, n_chunks, GDN_PREPARE_CHUNKS):
        _delta_prepare(range(c0, c0 + GDN_PREPARE_CHUNKS), ins, wq_ref, u_ref, aqk_ref, kd_ref, glast_ref)

    tn = (((0,), (0,)), ((), ()))
    n_chains = 2 * GDN_HEADS
    for ci, g0 in [(ci, g0) for ci in range(n_chunks) for g0 in range(0, n_chains, GDN_SCAN_CHAINS)]:
        group = range(g0, g0 + GDN_SCAN_CHAINS)
        chunk_of = {g: ci if g < GDN_HEADS else n_chunks - 1 - ci for g in group}
        wss = {g: _dot(wq_ref[chunk_of[g], g], s_ref[g].astype(BF16)) for g in group}
        vnbs = {g: (u_ref[chunk_of[g], g] - wss[g][:c]).astype(BF16) for g in group}
        for g in group:
            o = wss[g][c:] + _dot(aqk_ref[chunk_of[g], g], vnbs[g])
            h = g % GDN_HEADS
            outs[g // GDN_HEADS][0, pl.ds(chunk_of[g] * c, c), h * GDN_DV:(h + 1) * GDN_DV] = o.astype(BF16)
        for g in group:
            gl = GDN_HEADS + g % GDN_HEADS
            e_last = glast_ref[chunk_of[g], g // GDN_HEADS, 0:1, gl:gl + 1]
            s_ref[g] = s_ref[g] * e_last + lax.dot_general(kd_ref[chunk_of[g], g], vnbs[g], tn,
                                                           preferred_element_type=F32)


def _gated_delta(qkv, gates):
    b, t, _ = qkv.shape
    tb = GDN_BLOCK_TOKENS
    nblk = t // tb
    n_chunks = tb // GDN_CHUNK
    n_chains = 2 * GDN_HEADS
    fwd = lambda width: pl.BlockSpec((1, tb, width), lambda bi, i: (bi, i, 0))
    bwd = lambda width: pl.BlockSpec((1, tb, width), lambda bi, i: (bi, nblk - 1 - i, 0))
    gate_f = pl.BlockSpec((1, 1, tb, GDN_GATE_LANES), lambda bi, i: (0, bi, i, 0))
    gate_b = pl.BlockSpec((1, 1, tb, GDN_GATE_LANES), lambda bi, i: (1, bi, nblk - 1 - i, 0))
    out = jax.ShapeDtypeStruct((b, t, GDN_VW), BF16)
    return pl.pallas_call(
        _delta_kernel,
        grid=(b, nblk),
        in_specs=[fwd(GDN_CONV_CH), gate_f, bwd(GDN_CONV_CH), gate_b],
        out_specs=[fwd(GDN_VW), bwd(GDN_VW)],
        out_shape=[out, out],
        scratch_shapes=[pltpu.VMEM((n_chains, GDN_DK, GDN_DV), F32),
                        pltpu.VMEM((n_chunks, n_chains, 2 * GDN_CHUNK, GDN_DK), BF16),
                        pltpu.VMEM((n_chunks, n_chains, GDN_CHUNK, GDN_DV), F32),
                        pltpu.VMEM((n_chunks, n_chains, GDN_CHUNK, GDN_CHUNK), BF16),
                        pltpu.VMEM((n_chunks, n_chains, GDN_CHUNK, GDN_DK), BF16),
                        pltpu.VMEM((n_chunks, 2, 8, GDN_GATE_LANES), F32)],
        compiler_params=_cparams(("parallel", "arbitrary")),
        name="gated_delta",
    )(qkv, gates, qkv, gates)


def _out_kernel(of_ref, ob_ref, z_ref, h_ref, nw_ref, wo_ref, fw_ref, y_ref):
    def gated(hd):
        lanes = slice(hd * GDN_DV, (hd + 1) * GDN_DV)
        o = of_ref[:, lanes].astype(F32) + ob_ref[:, lanes].astype(F32)
        z = z_ref[:, lanes].astype(F32)
        on = o * _rms_scale(o) * nw_ref[...]
        return (on * _silu(z)).astype(BF16)

    h = h_ref[...]
    for hd in range(GDN_HEADS):
        h = h + _dot(gated(hd), wo_ref[hd * GDN_DV:(hd + 1) * GDN_DV, :])
    y_ref[...] = h * _rms_scale(h) * fw_ref[...]


def _out_proj(of2, ob2, z2, h2, nw, wo, fw):
    n = h2.shape[0]
    tm = TOKEN_TILE
    tok = lambda width: pl.BlockSpec((tm, width), lambda i: (i, 0))
    return pl.pallas_call(
        _out_kernel,
        grid=(n // tm,),
        in_specs=[tok(GDN_VW), tok(GDN_VW), tok(GDN_VW), tok(D_MODEL),
                  _const_spec((1, GDN_DV)), _const_spec(wo.shape), _const_spec((1, D_MODEL))],
        out_specs=tok(D_MODEL),
        out_shape=jax.ShapeDtypeStruct((n, D_MODEL), F32),
        compiler_params=_cparams(("parallel",)),
        name="out_proj",
    )(of2, ob2, z2, h2, nw, wo, fw)


def _trunk(x, p):
    b, t, _ = x.shape
    n = b * t
    x2 = x.reshape(n, D_MODEL)
    qg, kv = _na_in_proj(x2, p["ln0"], p["na_w_in"])
    seq = lambda a: a.reshape(b, t, a.shape[-1])
    og = _neighbourhood_attention(seq(qg), seq(kv), p["na_bias"])
    h1, qkv, z, graw = _mid_proj(x2, og.reshape(n, NA_WIDTH), p["na_w_out"], p["ln1"],
                                 p["w_qkv"], p["w_z"], p["w_gate"])
    qkv_n, gates = _conv_gates(seq(qkv), seq(graw), p["conv_w"], p["alog_l"], p["dtb_l"])
    o_f, o_b = _gated_delta(qkv_n, gates)
    y = _out_proj(o_f.reshape(n, GDN_VW), o_b.reshape(n, GDN_VW), z, h1,
                  p["gdn_norm_w"], p["gdn_w_out"], p["final_w"])
    return y.reshape(b, t, D_MODEL)


def kernel(x_prompt, x_sample, ln_w, na_w_in, na_rpb, na_w_out, gdn_w_in, gdn_conv_w, gdn_a_log,
           gdn_dt_bias, gdn_norm_w, gdn_w_out, final_norm_w):
    o1 = GDN_CONV_CH
    o2 = o1 + GDN_VW
    w_in = gdn_w_in[0]
    n_gate = 4 * GDN_HEADS
    lane_pad = lambda a: jnp.pad(a, ((0, 0), (0, GDN_GATE_LANES - a.shape[1])))
    shift = lambda a: jnp.pad(a.reshape(1, 2 * GDN_HEADS).astype(F32),
                              ((0, 0), (2 * GDN_HEADS, GDN_GATE_LANES - n_gate)))
    p = {
        "ln0": ln_w[0].reshape(1, D_MODEL),
        "ln1": ln_w[1].reshape(1, D_MODEL),
        "na_w_in": na_w_in[0].astype(BF16),
        "na_w_out": na_w_out[0].astype(BF16),
        "w_qkv": w_in[:, :o1].astype(BF16),
        "w_z": w_in[:, o1:o2].astype(BF16),
        "w_gate": lane_pad(w_in[:, o2:]).astype(BF16),
        "conv_w": jnp.pad(gdn_conv_w[0], ((0, 8 - GDN_CONV), (0, 0))),
        "alog_l": shift(gdn_a_log[0]),
        "dtb_l": shift(gdn_dt_bias[0]),
        "gdn_norm_w": gdn_norm_w[0].reshape(1, GDN_DV),
        "gdn_w_out": gdn_w_out[0].astype(BF16),
        "final_w": final_norm_w.reshape(1, D_MODEL),
    }
    p["na_bias"] = _na_bias_tables(na_rpb[0])
    return tuple(_trunk(x, p) for x in (x_prompt, x_sample))
```

```python
import functools

import numpy as np
import jax
import jax.numpy as jnp
from jax import lax
from jax.experimental import pallas as pl
from jax.experimental.pallas import tpu as pltpu

F32 = jnp.float32
BF16 = jnp.bfloat16

D_MODEL = 1024
GRID_W = 64
RMS_EPS = 1e-6
L2_EPS = 1e-6
NEG_INF = -1e30

NA_HEADS = 16
NA_HEAD_DIM = 64
NA_WIDTH = NA_HEADS * NA_HEAD_DIM
NA_WIN_H = 8
NA_WIN_W = 16
NA_ROWS_PER_BLOCK = 4
NA_WIN_TOKENS = NA_WIN_H * GRID_W
NA_PAIR_GROUP = 4

GDN_HEADS = 8
GDN_DK = 128
GDN_DV = 256
GDN_KW = GDN_HEADS * GDN_DK
GDN_VW = GDN_HEADS * GDN_DV
GDN_CONV = 5
GDN_CONV_CH = 2 * GDN_KW + GDN_VW
GDN_CHUNK = 64
GDN_PREPARE_CHUNKS = 2
GDN_SCAN_CHAINS = 16
GDN_GATE_LANES = 128
GDN_BLOCK_TOKENS = 512
CONV_BLOCK_TOKENS = 512
CONV_HALO = 16
CONV_GROUP_ROWS = 64
CONV_LANE_CHUNK = 512

TOKEN_TILE = 512
VMEM_LIMIT = 56 * 1024 * 1024


def _cparams(sem):
    return pltpu.CompilerParams(dimension_semantics=sem, vmem_limit_bytes=VMEM_LIMIT)


def _rms_scale(x):
    return lax.rsqrt(jnp.mean(x * x, axis=-1, keepdims=True) + RMS_EPS)


def _silu(x):
    return x * jax.nn.sigmoid(x)


def _const_spec(shape):
    nd = len(shape)
    return pl.BlockSpec(shape, lambda *_: (0,) * nd, pipeline_mode=pl.Buffered(1))


def _na_in_kernel(x_ref, lnw_ref, w_ref, qg_ref, kv_ref):
    x = x_ref[...]
    xn = (x * _rms_scale(x) * lnw_ref[...]).astype(BF16)
    dests = ((qg_ref, 0), (kv_ref, 0), (kv_ref, NA_WIDTH), (qg_ref, NA_WIDTH))
    for j, (o_ref, col) in enumerate(dests):
        y = jnp.dot(xn, w_ref[:, j * NA_WIDTH:(j + 1) * NA_WIDTH], preferred_element_type=F32)
        if j == 0:
            y = y * (NA_HEAD_DIM ** -0.5)
        o_ref[:, col:col + NA_WIDTH] = y.astype(o_ref.dtype)


def _na_in_proj(x2, lnw, w):
    n = x2.shape[0]
    tm = TOKEN_TILE
    tok = lambda width: pl.BlockSpec((tm, width), lambda i: (i, 0))
    out = jax.ShapeDtypeStruct((n, 2 * NA_WIDTH), BF16)
    return pl.pallas_call(
        _na_in_kernel,
        grid=(n // tm,),
        in_specs=[tok(D_MODEL), _const_spec((1, D_MODEL)), _const_spec((D_MODEL, 4 * NA_WIDTH))],
        out_specs=[tok(2 * NA_WIDTH)] * 2,
        out_shape=[out] * 2,
        compiler_params=_cparams(("parallel",)),
        name="na_in_proj",
    )(x2, lnw, w)


def _na_bias_tables(rpb):
    qc = np.arange(GRID_W)[:, None]
    kc = np.arange(GRID_W)[None, :]
    win_start = np.clip(qc - NA_WIN_W // 2, 0, GRID_W - NA_WIN_W)
    col_ok = (kc >= win_start) & (kc < win_start + NA_WIN_W)
    side = GRID_W - NA_WIN_W
    padded = jnp.pad(rpb.astype(F32), ((0, 0), (0, 0), (side, side)))
    toeplitz = jnp.stack([padded[:, :, GRID_W - 1 - c:2 * GRID_W - 1 - c] for c in range(GRID_W)],
                         axis=2)
    toeplitz = jnp.where(col_ok[None, None], toeplitz, NEG_INF)
    rb = NA_ROWS_PER_BLOCK
    d_all = list(range(rb)) + [NA_WIN_H // 2] * rb + [NA_WIN_H - rb + j for j in range(rb)]
    tables = []
    for d in d_all:
        t = toeplitz[:, NA_WIN_H - 1 - d:2 * NA_WIN_H - 1 - d]
        tables.append(t.transpose(0, 2, 1, 3).reshape(NA_HEADS, GRID_W, NA_WIN_TOKENS))
    return jnp.stack(tables)


def _na_kernel(qg_ref, kvp_ref, kvc_ref, kvn_ref, bias_ref, o_ref, kvcat_ref, *, rows):
    rb = NA_ROWS_PER_BLOCK
    blk_tokens = rb * GRID_W
    i = pl.program_id(0)
    kvcat_ref[0:blk_tokens, :] = kvp_ref[0]
    kvcat_ref[blk_tokens:2 * blk_tokens, :] = kvc_ref[0]
    kvcat_ref[2 * blk_tokens:3 * blk_tokens, :] = kvn_ref[0]

    lane = lax.broadcasted_iota(jnp.int32, (GRID_W, 2 * NA_HEAD_DIM), 1)
    first_half = lane < NA_HEAD_DIM
    r0 = i * rb
    nt = (((1,), (1,)), ((), ()))
    pair_lanes = [pl.ds(p * 2 * NA_HEAD_DIM, 2 * NA_HEAD_DIM) for p in range(NA_HEADS // 2)]
    pair_lanes_hi = [pl.ds(NA_WIDTH + p * 2 * NA_HEAD_DIM, 2 * NA_HEAD_DIM) for p in range(NA_HEADS // 2)]

    def key_rows(qi):
        rs = jnp.clip(r0 + qi - NA_WIN_H // 2, 0, rows - NA_WIN_H)
        return pl.ds(pl.multiple_of((rs - (r0 - rb)) * GRID_W, GRID_W), NA_WIN_TOKENS)

    for qi, p0 in [(qi, p0) for qi in range(rb) for p0 in range(0, NA_HEADS // 2, NA_PAIR_GROUP)]:
        pairs = range(p0, p0 + NA_PAIR_GROUP)
        qrows = pl.ds(qi * GRID_W, GRID_W)
        krows = key_rows(qi)
        ss = {}
        for p in pairs:
            qp = qg_ref[0, qrows, pair_lanes[p]]
            kw = kvcat_ref[krows, pair_lanes[p]]
            zero = jnp.zeros_like(qp)
            q2 = jnp.concatenate([jnp.where(first_half, qp, zero), jnp.where(first_half, zero, qp)], axis=0)
            s2 = lax.dot_general(q2, kw, nt, preferred_element_type=F32)
            for half in range(2):
                ss[2 * p + half] = s2[half * GRID_W:(half + 1) * GRID_W] + bias_ref[qi, 2 * p + half]
        es, ls = {}, {}
        for hd, s in ss.items():
            e = jnp.exp(s - jnp.max(s, axis=-1, keepdims=True))
            ls[hd] = jnp.sum(e, axis=-1, keepdims=True)
            es[hd] = e.astype(BF16)
        for p in pairs:
            vw = kvcat_ref[krows, pair_lanes_hi[p]]
            o2 = jnp.dot(jnp.concatenate([es[2 * p], es[2 * p + 1]], axis=0), vw, preferred_element_type=F32)
            o_half = [o2[half * GRID_W:(half + 1) * GRID_W] / ls[2 * p + half] for half in range(2)]
            o = jnp.where(first_half, o_half[0], o_half[1])
            g = qg_ref[0, qrows, pair_lanes_hi[p]].astype(F32)
            o_ref[0, qrows, pair_lanes[p]] = (o * _silu(g)).astype(o_ref.dtype)


def _neighbourhood_attention(qg, kv, bias):
    b, t, _ = qg.shape
    rows = t // GRID_W
    rb = NA_ROWS_PER_BLOCK
    assert rows % rb == 0 and rows >= 2 * NA_WIN_H
    nb = rows // rb
    bt = rb * GRID_W
    cur = lambda width: pl.BlockSpec((1, bt, width), lambda i, bi: (bi, i, 0))
    prev = pl.BlockSpec((1, bt, 2 * NA_WIDTH), lambda i, bi: (bi, jnp.maximum(i - 1, 0), 0))
    nxt = pl.BlockSpec((1, bt, 2 * NA_WIDTH), lambda i, bi: (bi, jnp.minimum(i + 1, nb - 1), 0))
    case = lambda i: jnp.where(i == 0, 0, jnp.where(i == nb - 1, 2, 1))
    bias_spec = pl.BlockSpec((rb, NA_HEADS, GRID_W, NA_WIN_TOKENS), lambda i, bi: (case(i), 0, 0, 0))
    return pl.pallas_call(
        functools.partial(_na_kernel, rows=rows),
        grid=(nb, b),
        in_specs=[cur(2 * NA_WIDTH), prev, cur(2 * NA_WIDTH), nxt, bias_spec],
        out_specs=cur(NA_WIDTH),
        out_shape=jax.ShapeDtypeStruct((b, t, NA_WIDTH), BF16),
        scratch_shapes=[pltpu.VMEM((3 * bt, 2 * NA_WIDTH), BF16)],
        compiler_params=_cparams(("arbitrary", "arbitrary")),
        name="neighbourhood_attention",
    )(qg, kv, kv, kv, bias)


def _mid_kernel(x_ref, og_ref, wo_ref, lnw_ref, wqkv_ref, wz_ref, wg_ref,
                h_ref, qkv_ref, z_ref, gate_ref):
    h = x_ref[...] + jnp.dot(og_ref[...], wo_ref[...], preferred_element_type=F32)
    h_ref[...] = h
    hn = (h * _rms_scale(h) * lnw_ref[...]).astype(BF16)
    qkv_ref[...] = jnp.dot(hn, wqkv_ref[...], preferred_element_type=F32).astype(qkv_ref.dtype)
    z_ref[...] = jnp.dot(hn, wz_ref[...], preferred_element_type=F32).astype(z_ref.dtype)
    gate_ref[...] = jnp.dot(hn, wg_ref[...], preferred_element_type=F32)


def _mid_proj(x2, og2, wo, lnw, wqkv, wz, wg):
    n = x2.shape[0]
    tm = TOKEN_TILE // 2
    tok = lambda width: pl.BlockSpec((tm, width), lambda i: (i, 0))
    return pl.pallas_call(
        _mid_kernel,
        grid=(n // tm,),
        in_specs=[tok(D_MODEL), tok(NA_WIDTH), _const_spec(wo.shape), _const_spec((1, D_MODEL)),
                  _const_spec(wqkv.shape), _const_spec(wz.shape), _const_spec(wg.shape)],
        out_specs=[tok(D_MODEL), tok(GDN_CONV_CH), tok(GDN_VW), tok(GDN_GATE_LANES)],
        out_shape=[jax.ShapeDtypeStruct((n, D_MODEL), F32),
                   jax.ShapeDtypeStruct((n, GDN_CONV_CH), BF16),
                   jax.ShapeDtypeStruct((n, GDN_VW), BF16),
                   jax.ShapeDtypeStruct((n, GDN_GATE_LANES), F32)],
        compiler_params=_cparams(("parallel",)),
        name="mid_proj",
    )(x2, og2, wo, lnw, wqkv, wz, wg)


def _conv_kernel(prev_ref, cur_ref, next_ref, cw_ref, graw_ref, alog_ref, dtb_ref,
                 qkv_ref, gate_ref, ext_ref):
    i = pl.program_id(1)
    nblk = pl.num_programs(1)
    tb = CONV_BLOCK_TOKENS
    pad = GDN_CONV // 2
    halo = CONV_HALO
    grp = CONV_GROUP_ROWS
    ext_ref[0:halo, :] = jnp.where(i > 0, prev_ref[0], jnp.zeros_like(prev_ref[0]))
    ext_ref[halo:halo + tb, :] = cur_ref[0]
    ext_ref[halo + tb:, :] = jnp.where(i < nblk - 1, next_ref[0], jnp.zeros_like(next_ref[0]))

    taps = [j for j in range(GDN_CONV) if j != pad]
    srow = lax.broadcasted_iota(jnp.int32, (len(taps) * grp, grp + 2 * halo), 0)
    scol = lax.broadcasted_iota(jnp.int32, (len(taps) * grp, grp + 2 * halo), 1)
    tap_idx = srow // grp
    offset = jnp.where(tap_idx < pad, tap_idx - pad, tap_idx - pad + 1)
    shift = jnp.where(scol == halo + srow % grp + offset, 1.0, 0.0).astype(BF16)

    lane_chunk = CONV_LANE_CHUNK
    work = [(r, lc) for r in range(tb // grp) for lc in range(GDN_CONV_CH // lane_chunk)]

    def shifted(item):
        r, lc = item
        return _dot(shift, ext_ref[r * grp:r * grp + grp + 2 * halo, lc * lane_chunk:(lc + 1) * lane_chunk])

    pending = shifted(work[0])
    for n, (r, lc) in enumerate(work):
        y = pending
        if n + 1 < len(work):
            pending = shifted(work[n + 1])
        lanes = slice(lc * lane_chunk, (lc + 1) * lane_chunk)
        rows = slice(r * grp, (r + 1) * grp)
        acc = ext_ref[halo + r * grp:halo + (r + 1) * grp, lanes].astype(F32) * cw_ref[pad:pad + 1, lanes]
        for m, j in enumerate(taps):
            acc = acc + y[m * grp:(m + 1) * grp] * cw_ref[j:j + 1, lanes]
        act = _silu(acc)
        for sub in range(lane_chunk // GDN_DK):
            lo = lc * lane_chunk + sub * GDN_DK
            a = act[:, sub * GDN_DK:(sub + 1) * GDN_DK]
            if lo < 2 * GDN_KW:
                scale = GDN_DK ** -0.5 if lo < GDN_KW else 1.0
                a = a * (lax.rsqrt(jnp.sum(a * a, axis=-1, keepdims=True) + L2_EPS) * scale)
            qkv_ref[0, rows, lo:lo + GDN_DK] = a.astype(qkv_ref.dtype)

    x = graw_ref[0]
    lane = lax.broadcasted_iota(jnp.int32, x.shape, 1)
    beta = jax.nn.sigmoid(x)
    decay = -jnp.exp(alog_ref[...]) * jax.nn.softplus(x + dtb_ref[...])
    for d in range(2):
        bsh = beta if d == 0 else pltpu.roll(beta, GDN_GATE_LANES - d * GDN_HEADS, 1)
        gsh = pltpu.roll(decay, GDN_GATE_LANES - (1 + d) * GDN_HEADS, 1)
        gate_ref[d, 0] = jnp.where(lane < GDN_HEADS, bsh, jnp.where(lane < 2 * GDN_HEADS, gsh, 0.0))


def _conv_gates(qkv, graw, conv_w, alog_l, dtb_l):
    b, t, _ = qkv.shape
    tb = CONV_BLOCK_TOKENS
    nblk = t // tb
    hb = tb // CONV_HALO
    nh = t // CONV_HALO
    cur = lambda width: pl.BlockSpec((1, tb, width), lambda bi, i: (bi, i, 0))
    prev = pl.BlockSpec((1, CONV_HALO, GDN_CONV_CH), lambda bi, i: (bi, jnp.maximum(i * hb - 1, 0), 0))
    nxt = pl.BlockSpec((1, CONV_HALO, GDN_CONV_CH),
                       lambda bi, i: (bi, jnp.minimum((i + 1) * hb, nh - 1), 0))
    return pl.pallas_call(
        _conv_kernel,
        grid=(b, nblk),
        in_specs=[prev, cur(GDN_CONV_CH), nxt, _const_spec(conv_w.shape), cur(GDN_GATE_LANES),
                  _const_spec((1, GDN_GATE_LANES)), _const_spec((1, GDN_GATE_LANES))],
        out_specs=[cur(GDN_CONV_CH),
                   pl.BlockSpec((2, 1, tb, GDN_GATE_LANES), lambda bi, i: (0, bi, i, 0))],
        out_shape=[jax.ShapeDtypeStruct((b, t, GDN_CONV_CH), BF16),
                   jax.ShapeDtypeStruct((2, b, t, GDN_GATE_LANES), F32)],
        scratch_shapes=[pltpu.VMEM((tb + 2 * CONV_HALO, GDN_CONV_CH), BF16)],
        compiler_params=_cparams(("parallel", "arbitrary")),
        name="conv_gates",
    )(qkv, qkv, qkv, conv_w, graw, alog_l, dtb_l)


def _split3(x):
    hi = x.astype(BF16)
    r1 = x - hi.astype(F32)
    mid = r1.astype(BF16)
    lo = (r1 - mid.astype(F32)).astype(BF16)
    return hi, mid, lo


def _dot(a, b):
    return jnp.dot(a, b, preferred_element_type=F32)


def _unit_triangular_inverses(a_list):
    c = a_list[0].shape[0]
    row = lax.broadcasted_iota(jnp.int32, (c, 2 * c), 0)
    col = lax.broadcasted_iota(jnp.int32, (c, 2 * c), 1)
    x_half = col >= c
    eye = jnp.where(col == row + c, 1.0, 0.0)
    zs = [jnp.where(x_half, eye, -jnp.concatenate([a, a], axis=1)) for a in a_list]
    for _ in range(int(np.log2(c))):
        zbs = [z.astype(BF16) for z in zs]
        zs = [_dot(zb[:, :c], zb) + jnp.where(x_half, z, 0.0) for z, zb in zip(zs, zbs)]
    return [z[:, c:] for z in zs]


def _delta_prepare(chunks, ins, wq_ref, u_ref, aqk_ref, kd_ref, glast_ref):
    c = GDN_CHUNK
    row = lax.broadcasted_iota(jnp.int32, (c, c), 0)
    col = lax.broadcasted_iota(jnp.int32, (c, c), 1)
    nt = (((1,), (1,)), ((), ()))
    chains = []
    for cc in chunks:
        rows = pl.ds(cc * c, c)
        for direction, (qkv_ref, gate_ref) in enumerate(ins):
            causal = (row >= col) if direction == 0 else (row <= col)
            strict = (row > col) if direction == 0 else (row < col)
            tri = jnp.where(causal, 1.0, 0.0).astype(BF16)
            gates = gate_ref[0, 0, rows, :]
            hi, mid, lo = _split3(gates)
            gcum = _dot(tri, hi) + _dot(tri, mid) + _dot(tri, lo)
            gcum_t = gcum.T
            gtot = jnp.sum(gates, axis=0, keepdims=True)
            glast_ref[cc, direction] = jnp.broadcast_to(jnp.exp(gtot), (8, GDN_GATE_LANES))
            for h in range(GDN_HEADS):
                gl = GDN_HEADS + h
                chains.append(dict(
                    slot=(cc, direction * GDN_HEADS + h), causal=causal, strict=strict,
                    beta=gates[:, h:h + 1], gc=gcum[:, gl:gl + 1], gr=gcum_t[gl:gl + 1, :],
                    g_last=gtot[:, gl:gl + 1],
                    q=qkv_ref[0, rows, h * GDN_DK:(h + 1) * GDN_DK],
                    k=qkv_ref[0, rows, GDN_KW + h * GDN_DK:GDN_KW + (h + 1) * GDN_DK],
                    v=qkv_ref[0, rows, 2 * GDN_KW + h * GDN_DV:2 * GDN_KW + (h + 1) * GDN_DV]))
    for ch in chains:
        ch["kf"] = ch["k"].astype(F32)
        ch["kbeta"] = ch["kf"] * ch["beta"]
    kqs = [lax.dot_general(jnp.concatenate([ch["kbeta"].astype(BF16), ch["q"]], axis=0), ch["k"], nt,
                           preferred_element_type=F32) for ch in chains]
    a_list = []
    for ch, kq in zip(chains, kqs):
        decay = jnp.where(ch["causal"], jnp.exp(jnp.where(ch["causal"], ch["gc"] - ch["gr"], 0.0)), 0.0)
        a_list.append(jnp.where(ch["strict"], kq[:c] * decay, 0.0))
        ch["aqk"] = (kq[c:] * decay).astype(BF16)
    tinvs = _unit_triangular_inverses(a_list)
    rhss = []
    for ch in chains:
        ch["eg"] = jnp.exp(ch["gc"])
        rhss.append(jnp.concatenate([(ch["kbeta"] * ch["eg"]).astype(BF16),
                                     (ch["v"].astype(F32) * ch["beta"]).astype(BF16)], axis=1))
    wus = [_dot(t.astype(BF16), rhs) for t, rhs in zip(tinvs, rhss)]
    for ch, wu in zip(chains, wus):
        cc, g = ch["slot"]
        wq_ref[cc, g, 0:c, :] = wu[:, :GDN_DK].astype(BF16)
        wq_ref[cc, g, c:2 * c, :] = (ch["q"].astype(F32) * ch["eg"]).astype(BF16)
        u_ref[cc, g] = wu[:, GDN_DK:]
        aqk_ref[cc, g] = ch["aqk"]
        kd_ref[cc, g] = (ch["kf"] * jnp.exp(ch["g_last"] - ch["gc"])).astype(BF16)


def _delta_kernel(qkvf_ref, gf_ref, qkvb_ref, gb_ref,
                  of_ref, ob_ref, s_ref, wq_ref, u_ref, aqk_ref, kd_ref, glast_ref):
    i = pl.program_id(1)
    c = GDN_CHUNK
    n_chunks = GDN_BLOCK_TOKENS // c
    ins = ((qkvf_ref, gf_ref), (qkvb_ref, gb_ref))
    outs = (of_ref, ob_ref)

    @pl.when(i == 0)
    def _():
        s_ref[...] = jnp.zeros_like(s_ref)

    for c0 in range(0, n_chunks, GDN_PREPARE_CHUNKS):
        _delta_prepare(range(c0, c0 + GDN_PREPARE_CHUNKS), ins, wq_ref, u_ref, aqk_ref, kd_ref, glast_ref)

    tn = (((0,), (0,)), ((), ()))
    n_chains = 2 * GDN_HEADS
    for ci, g0 in [(ci, g0) for ci in range(n_chunks) for g0 in range(0, n_chains, GDN_SCAN_CHAINS)]:
        group = range(g0, g0 + GDN_SCAN_CHAINS)
        chunk_of = {g: ci if g < GDN_HEADS else n_chunks - 1 - ci for g in group}
        wss = {g: _dot(wq_ref[chunk_of[g], g], s_ref[g].astype(BF16)) for g in group}
        vnbs = {g: (u_ref[chunk_of[g], g] - wss[g][:c]).astype(BF16) for g in group}
        for g in group:
            o = wss[g][c:] + _dot(aqk_ref[chunk_of[g], g], vnbs[g])
            h = g % GDN_HEADS
            outs[g // GDN_HEADS][0, pl.ds(chunk_of[g] * c, c), h * GDN_DV:(h + 1) * GDN_DV] = o.astype(BF16)
        for g in group:
            gl = GDN_HEADS + g % GDN_HEADS
            e_last = glast_ref[chunk_of[g], g // GDN_HEADS, 0:1, gl:gl + 1]
            s_ref[g] = s_ref[g] * e_last + lax.dot_general(kd_ref[chunk_of[g], g], vnbs[g], tn,
                                                           preferred_element_type=F32)


def _gated_delta(qkv, gates):
    b, t, _ = qkv.shape
    tb = GDN_BLOCK_TOKENS
    nblk = t // tb
    n_chunks = tb // GDN_CHUNK
    n_chains = 2 * GDN_HEADS
    fwd = lambda width: pl.BlockSpec((1, tb, width), lambda bi, i: (bi, i, 0))
    bwd = lambda width: pl.BlockSpec((1, tb, width), lambda bi, i: (bi, nblk - 1 - i, 0))
    gate_f = pl.BlockSpec((1, 1, tb, GDN_GATE_LANES), lambda bi, i: (0, bi, i, 0))
    gate_b = pl.BlockSpec((1, 1, tb, GDN_GATE_LANES), lambda bi, i: (1, bi, nblk - 1 - i, 0))
    out = jax.ShapeDtypeStruct((b, t, GDN_VW), BF16)
    return pl.pallas_call(
        _delta_kernel,
        grid=(b, nblk),
        in_specs=[fwd(GDN_CONV_CH), gate_f, bwd(GDN_CONV_CH), gate_b],
        out_specs=[fwd(GDN_VW), bwd(GDN_VW)],
        out_shape=[out, out],
        scratch_shapes=[pltpu.VMEM((n_chains, GDN_DK, GDN_DV), F32),
                        pltpu.VMEM((n_chunks, n_chains, 2 * GDN_CHUNK, GDN_DK), BF16),
                        pltpu.VMEM((n_chunks, n_chains, GDN_CHUNK, GDN_DV), F32),
                        pltpu.VMEM((n_chunks, n_chains, GDN_CHUNK, GDN_CHUNK), BF16),
                        pltpu.VMEM((n_chunks, n_chains, GDN_CHUNK, GDN_DK), BF16),
                        pltpu.VMEM((n_chunks, 2, 8, GDN_GATE_LANES), F32)],
        compiler_params=_cparams(("parallel", "arbitrary")),
        name="gated_delta",
    )(qkv, gates, qkv, gates)


def _out_kernel(of_ref, ob_ref, z_ref, h_ref, nw_ref, wo_ref, fw_ref, y_ref):
    def gated(hd):
        lanes = slice(hd * GDN_DV, (hd + 1) * GDN_DV)
        o = of_ref[:, lanes].astype(F32) + ob_ref[:, lanes].astype(F32)
        z = z_ref[:, lanes].astype(F32)
        on = o * _rms_scale(o) * nw_ref[...]
        return (on * _silu(z)).astype(BF16)

    h = h_ref[...]
    for hd in range(GDN_HEADS):
        h = h + _dot(gated(hd), wo_ref[hd * GDN_DV:(hd + 1) * GDN_DV, :])
    y_ref[...] = h * _rms_scale(h) * fw_ref[...]


def _out_proj(of2, ob2, z2, h2, nw, wo, fw):
    n = h2.shape[0]
    tm = TOKEN_TILE
    tok = lambda width: pl.BlockSpec((tm, width), lambda i: (i, 0))
    return pl.pallas_call(
        _out_kernel,
        grid=(n // tm,),
        in_specs=[tok(GDN_VW), tok(GDN_VW), tok(GDN_VW), tok(D_MODEL),
                  _const_spec((1, GDN_DV)), _const_spec(wo.shape), _const_spec((1, D_MODEL))],
        out_specs=tok(D_MODEL),
        out_shape=jax.ShapeDtypeStruct((n, D_MODEL), F32),
        compiler_params=_cparams(("parallel",)),
        name="out_proj",
    )(of2, ob2, z2, h2, nw, wo, fw)


def _trunk(x, p):
    b, t, _ = x.shape
    n = b * t
    x2 = x.reshape(n, D_MODEL)
    qg, kv = _na_in_proj(x2, p["ln0"], p["na_w_in"])
    seq = lambda a: a.reshape(b, t, a.shape[-1])
    og = _neighbourhood_attention(seq(qg), seq(kv), p["na_bias"])
    h1, qkv, z, graw = _mid_proj(x2, og.reshape(n, NA_WIDTH), p["na_w_out"], p["ln1"],
                                 p["w_qkv"], p["w_z"], p["w_gate"])
    qkv_n, gates = _conv_gates(seq(qkv), seq(graw), p["conv_w"], p["alog_l"], p["dtb_l"])
    o_f, o_b = _gated_delta(qkv_n, gates)
    y = _out_proj(o_f.reshape(n, GDN_VW), o_b.reshape(n, GDN_VW), z, h1,
                  p["gdn_norm_w"], p["gdn_w_out"], p["final_w"])
    return y.reshape(b, t, D_MODEL)


def kernel(x_prompt, x_sample, ln_w, na_w_in, na_rpb, na_w_out, gdn_w_in, gdn_conv_w, gdn_a_log,
           gdn_dt_bias, gdn_norm_w, gdn_w_out, final_norm_w):
    o1 = GDN_CONV_CH
    o2 = o1 + GDN_VW
    w_in = gdn_w_in[0]
    n_gate = 4 * GDN_HEADS
    lane_pad = lambda a: jnp.pad(a, ((0, 0), (0, GDN_GATE_LANES - a.shape[1])))
    shift = lambda a: jnp.pad(a.reshape(1, 2 * GDN_HEADS).astype(F32),
                              ((0, 0), (2 * GDN_HEADS, GDN_GATE_LANES - n_gate)))
    p = {
        "ln0": ln_w[0].reshape(1, D_MODEL),
        "ln1": ln_w[1].reshape(1, D_MODEL),
        "na_w_in": na_w_in[0].astype(BF16),
        "na_w_out": na_w_out[0].astype(BF16),
        "w_qkv": w_in[:, :o1].astype(BF16),
        "w_z": w_in[:, o1:o2].astype(BF16),
        "w_gate": lane_pad(w_in[:, o2:]).astype(BF16),
        "conv_w": jnp.pad(gdn_conv_w[0], ((0, 8 - GDN_CONV), (0, 0))),
        "alog_l": shift(gdn_a_log[0]),
        "dtb_l": shift(gdn_dt_bias[0]),
        "gdn_norm_w": gdn_norm_w[0].reshape(1, GDN_DV),
        "gdn_w_out": gdn_w_out[0].astype(BF16),
        "final_w": final_norm_w.reshape(1, D_MODEL),
    }
    p["na_bias"] = _na_bias_tables(na_rpb[0])
    return tuple(_trunk(x, p) for x in (x_prompt, x_sample))
```
